```python
import jax, jax.numpy as jnp
from jax import lax
import numpy as np

D_MODEL = 2048
BATCH = 2
SEQ = 4096
DEPTH = 2
DEC_BATCH = 4
DEC_SEQ = 8192
PAST_LEN = 128

N_MIXERS = 2
N_ATTN_LAYERS = (DEPTH + 1) // 2
N_GLA_LAYERS = DEPTH // 2
HEAD_DIM = 128
N_Q_HEADS = D_MODEL // HEAD_DIM
N_KV_HEADS = N_Q_HEADS // 4
Q_PER_KV = N_Q_HEADS // N_KV_HEADS
WINDOW = 128
ATTN_BLOCK = 128
ATTN_IN = (N_Q_HEADS + 2 * N_KV_HEADS) * HEAD_DIM
GLA_HEADS = 4
GLA_DK = D_MODEL // 2 // GLA_HEADS
GLA_DV = D_MODEL // GLA_HEADS
GLA_RANK = 16
GLA_TAU = 16.0
GLA_CHUNK = 64
GLA_SPLITS = (GLA_HEADS * GLA_DK, 2 * GLA_HEADS * GLA_DK, 2 * GLA_HEADS * GLA_DK + GLA_HEADS * GLA_DV,
              2 * GLA_HEADS * GLA_DK + 2 * GLA_HEADS * GLA_DV,
              2 * GLA_HEADS * GLA_DK + 2 * GLA_HEADS * GLA_DV + GLA_RANK)
GLA_IN = 2 * GLA_HEADS * GLA_DK + 2 * GLA_HEADS * GLA_DV + 2 * GLA_RANK
N_EXPERTS = 64
TOP_K = 8
N_GROUPS = 8
TOPK_GROUPS = 4
EXPERT_DIM = 512
SHARED_DIM = 512
ROUTED_SCALE = 2.5
MOE_BLOCK = 256
EPS = 1e-6

kernel_name = "hybrid_bidir_swa_gla_moe_adaln"


def rms_norm(x, g):
    xf = x.astype(jnp.float32)
    y = xf * lax.rsqrt(jnp.mean(xf * xf, axis=-1, keepdims=True) + EPS)
    return (y * g.astype(jnp.float32)).astype(x.dtype)


def alibi_slopes():
    return jnp.exp2(-8.0 * jnp.arange(1, N_Q_HEADS + 1, dtype=jnp.float32) / N_Q_HEADS)


def windowed_attention(q, k, v, sink):
    B, S = q.shape[:2]
    nb = S // ATTN_BLOCK
    span = ATTN_BLOCK + 2 * WINDOW
    pad = ((0, 0), (WINDOW, WINDOW), (0, 0), (0, 0))
    kp = jnp.pad(k, pad)
    vp = jnp.pad(v, pad)
    qb = q.reshape(B, nb, ATTN_BLOCK, N_KV_HEADS, Q_PER_KV, HEAD_DIM)
    t = jnp.arange(ATTN_BLOCK)[:, None]
    j = jnp.arange(span)[None, :]
    dist = jnp.abs(t - j + WINDOW)
    in_window = dist <= WINDOW
    slopes = alibi_slopes().reshape(N_KV_HEADS, Q_PER_KV)
    alibi = -slopes[:, :, None, None] * dist.astype(jnp.float32)
    sink_l = jnp.broadcast_to(sink.astype(jnp.float32).reshape(1, N_KV_HEADS, Q_PER_KV, 1, 1),
                              (B, N_KV_HEADS, Q_PER_KV, ATTN_BLOCK, 1))
    scale = HEAD_DIM ** -0.5

    def block(i):
        start = i * ATTN_BLOCK
        qi = lax.dynamic_index_in_dim(qb, i, axis=1, keepdims=False)
        ki = lax.dynamic_slice_in_dim(kp, start, span, axis=1)
        vi = lax.dynamic_slice_in_dim(vp, start, span, axis=1)
        key_pos = start - WINDOW + jnp.arange(span)
        valid = in_window & ((key_pos >= 0) & (key_pos < S))[None, :]
        s = jnp.einsum('bqkgd,bskd->bkgqs', qi, ki).astype(jnp.float32) * scale + alibi
        s = jnp.where(valid, s, -jnp.inf)
        s = jnp.concatenate([s, sink_l], axis=-1)
        p = jax.nn.softmax(s, axis=-1)[..., :span]
        return jnp.einsum('bkgqs,bskd->bqkgd', p.astype(vi.dtype), vi)

    o = lax.map(block, jnp.arange(nb))
    return jnp.moveaxis(o, 0, 1).reshape(B, S, N_Q_HEADS * HEAD_DIM)


def attention_mixer(h, w_in, q_gain, k_gain, sink, w_out):
    B, S, _ = h.shape
    qkv = h @ w_in
    q, k, v = jnp.split(qkv, [N_Q_HEADS * HEAD_DIM, (N_Q_HEADS + N_KV_HEADS) * HEAD_DIM], axis=-1)
    q = rms_norm(q.reshape(B, S, N_KV_HEADS, Q_PER_KV, HEAD_DIM), q_gain)
    k = rms_norm(k.reshape(B, S, N_KV_HEADS, HEAD_DIM), k_gain)
    v = v.reshape(B, S, N_KV_HEADS, HEAD_DIM)
    o = windowed_attention(q, k, v, sink)
    return o @ w_out


def gla_chunked(q, k, v, log_a, strict):
    B, H, S, DK = q.shape
    DV = v.shape[-1]
    n = S // GLA_CHUNK
    ch = lambda a: a.reshape(B, H, n, GLA_CHUNK, a.shape[-1]).astype(jnp.float32)
    q, k, v, la = ch(q), ch(k), ch(v), ch(log_a)
    b = jnp.cumsum(la, axis=3)
    b_last = b[:, :, :, -1:, :]
    q_dec = q * jnp.exp(b)
    k_inv = k * jnp.exp(-b)
    k_end = k * jnp.exp(b_last - b)
    mask = jnp.tril(jnp.ones((GLA_CHUNK, GLA_CHUNK), dtype=bool), -1 if strict else 0)
    a = jnp.where(mask, jnp.einsum('bhncd,bhnsd->bhncs', q_dec, k_inv), 0.0)
    o_intra = jnp.einsum('bhncs,bhnsv->bhncv', a, v)
    chunk_decay = jnp.exp(b_last[:, :, :, 0, :])

    def step(state, inp):
        q_i, k_i, v_i, d_i = inp
        o_i = jnp.einsum('bhcd,bhdv->bhcv', q_i, state)
        state = state * d_i[..., None] + jnp.einsum('bhcd,bhcv->bhdv', k_i, v_i)
        return state, o_i

    xs = (jnp.moveaxis(q_dec, 2, 0), jnp.moveaxis(k_end, 2, 0), jnp.moveaxis(v, 2, 0),
          jnp.moveaxis(chunk_decay, 2, 0))
    _, o_inter = lax.scan(step, jnp.zeros((B, H, DK, DV), jnp.float32), xs)
    o = o_intra + jnp.moveaxis(o_inter, 0, 2)
    return o.reshape(B, H, S, DV)


def gla_mixer(h, w_in, w_g2_f, b_g_f, w_g2_b, b_g_b, norm_g, w_out):
    B, S, _ = h.shape
    proj = h @ w_in
    q, k, v, r, lr_f, lr_b = jnp.split(proj, list(GLA_SPLITS), axis=-1)
    heads = lambda a, d: a.reshape(B, S, GLA_HEADS, d).transpose(0, 2, 1, 3)
    q = heads(q, GLA_DK) * (GLA_DK ** -0.5)
    k = heads(k, GLA_DK)
    v = heads(v, GLA_DV)
    log_af = heads(jax.nn.log_sigmoid((lr_f @ w_g2_f + b_g_f).astype(jnp.float32)) / GLA_TAU, GLA_DK)
    log_ab = heads(jax.nn.log_sigmoid((lr_b @ w_g2_b + b_g_b).astype(jnp.float32)) / GLA_TAU, GLA_DK)
    flip = lambda a: jnp.flip(a, axis=2)
    o_f = gla_chunked(q, k, v, log_af, strict=False)
    o_b = flip(gla_chunked(flip(q), flip(k), flip(v), flip(log_ab), strict=True))
    o = rms_norm(o_f + o_b, norm_g)
    o = o.transpose(0, 2, 1, 3).reshape(B, S, GLA_HEADS * GLA_DV)
    o = (o * jax.nn.silu(r.astype(jnp.float32))).astype(h.dtype)
    return o @ w_out


def route(h_flat, w_router, bias):
    T = h_flat.shape[0]
    scores = jax.nn.sigmoid((h_flat @ w_router).astype(jnp.float32))
    biased = scores + bias.astype(jnp.float32)
    grp = biased.reshape(T, N_GROUPS, N_EXPERTS // N_GROUPS)
    grp_score = jnp.sum(lax.top_k(grp, 2)[0], axis=-1)
    _, top_groups = lax.top_k(grp_score, TOPK_GROUPS)
    gmask = jnp.any(top_groups[..., None] == jnp.arange(N_GROUPS), axis=-2)
    cand = jnp.where(jnp.repeat(gmask, N_EXPERTS // N_GROUPS, axis=-1), biased, -jnp.inf)
    _, idx = lax.top_k(cand, TOP_K)
    w = jnp.take_along_axis(scores, idx, axis=-1)
    w = w / jnp.sum(w, axis=-1, keepdims=True) * ROUTED_SCALE
    return idx, w


def routed_experts(h_flat, idx, w, w_gate, w_up, w_down):
    T, D = h_flat.shape
    P = T * TOP_K
    n_blocks = -(-(P + N_EXPERTS * (MOE_BLOCK - 1)) // MOE_BLOCK)
    cap = n_blocks * MOE_BLOCK
    flat_e = idx.reshape(P)
    order = jnp.argsort(flat_e)
    sorted_e = flat_e[order]
    counts = jnp.bincount(flat_e, length=N_EXPERTS)
    padded = (counts + MOE_BLOCK - 1) // MOE_BLOCK * MOE_BLOCK
    padded_end = jnp.cumsum(padded)
    start_pad = padded_end - padded
    start_sorted = jnp.cumsum(counts) - counts
    dest = start_pad[sorted_e] + jnp.arange(P) - start_sorted[sorted_e]
    slot_tok = jnp.full((cap,), T, jnp.int32).at[dest].set((order // TOP_K).astype(jnp.int32))
    slot_w = jnp.zeros((cap,), w.dtype).at[dest].set(w.reshape(P)[order])
    block_expert = jnp.minimum(
        jnp.searchsorted(padded_end, jnp.arange(n_blocks) * MOE_BLOCK, side='right'), N_EXPERTS - 1)
    h_pad = jnp.concatenate([h_flat, jnp.zeros((1, D), h_flat.dtype)], axis=0)

    def block(args):
        tok, gate, e = args
        xb = h_pad[tok]
        a = jax.nn.silu(xb @ w_gate[e]) * (xb @ w_up[e])
        return (a @ w_down[e]) * gate[:, None].astype(xb.dtype)

    y_slots = lax.map(block, (slot_tok.reshape(n_blocks, MOE_BLOCK),
                              slot_w.reshape(n_blocks, MOE_BLOCK), block_expert))
    y = jnp.zeros((T + 1, D), y_slots.dtype).at[slot_tok].add(y_slots.reshape(cap, D))
    return y[:T]


def moe(h, w_router, router_bias, w_gate, w_up, w_down, s_gate, s_up, s_down):
    B, S, D = h.shape
    hf = h.reshape(B * S, D)
    idx, w = route(hf, w_router, router_bias)
    y = routed_experts(hf, idx, w, w_gate, w_up, w_down)
    y = y + (jax.nn.silu(hf @ s_gate) * (hf @ s_up)) @ s_down
    return y.reshape(B, S, D)


def setup_inputs(seed: int = 0) -> dict:
    key = jax.random.key(seed)
    ks = iter(jax.random.split(key, 40))
    nrm = lambda shape, s: jax.random.normal(next(ks), shape, jnp.float32) * s
    D = D_MODEL
    return {
        "x_prompt": nrm((BATCH, SEQ, D), 1.0),
        "x_sample": nrm((DEC_BATCH, DEC_SEQ, D), 1.0),
        "c_prompt": nrm((BATCH, D), 1.0),
        "c_sample": nrm((DEC_BATCH, D), 1.0),
        "ada_w": nrm((DEPTH, D, 6 * D), 0.5 * D ** -0.5),
        "ada_b": nrm((DEPTH, 6 * D), 0.02),
        "norm_mix": 1.0 + nrm((DEPTH, D), 0.02),
        "norm_ffn": 1.0 + nrm((DEPTH, D), 0.02),
        "attn_w_in": nrm((N_ATTN_LAYERS, D, ATTN_IN), D ** -0.5),
        "attn_q_norm": 1.0 + nrm((N_ATTN_LAYERS, HEAD_DIM), 0.02),
        "attn_k_norm": 1.0 + nrm((N_ATTN_LAYERS, HEAD_DIM), 0.02),
        "attn_sink": nrm((N_ATTN_LAYERS, N_Q_HEADS), 1.0),
        "attn_w_out": nrm((N_ATTN_LAYERS, N_Q_HEADS * HEAD_DIM, D), (N_Q_HEADS * HEAD_DIM) ** -0.5),
        "gla_w_in": nrm((N_GLA_LAYERS, D, GLA_IN), D ** -0.5),
        "gla_w_g2_f": nrm((N_GLA_LAYERS, GLA_RANK, GLA_HEADS * GLA_DK), GLA_RANK ** -0.5),
        "gla_b_g_f": nrm((N_GLA_LAYERS, GLA_HEADS * GLA_DK), 0.1),
        "gla_w_g2_b": nrm((N_GLA_LAYERS, GLA_RANK, GLA_HEADS * GLA_DK), GLA_RANK ** -0.5),
        "gla_b_g_b": nrm((N_GLA_LAYERS, GLA_HEADS * GLA_DK), 0.1),
        "gla_norm": 1.0 + nrm((N_GLA_LAYERS, GLA_DV), 0.02),
        "gla_w_out": nrm((N_GLA_LAYERS, GLA_HEADS * GLA_DV, D), (GLA_HEADS * GLA_DV) ** -0.5),
        "router_w": nrm((DEPTH, D, N_EXPERTS), D ** -0.5),
        "router_bias": nrm((DEPTH, N_EXPERTS), 0.01),
        "exp_w_gate": nrm((DEPTH, N_EXPERTS, D, EXPERT_DIM), D ** -0.5),
        "exp_w_up": nrm((DEPTH, N_EXPERTS, D, EXPERT_DIM), D ** -0.5),
        "exp_w_down": nrm((DEPTH, N_EXPERTS, EXPERT_DIM, D), EXPERT_DIM ** -0.5),
        "sh_w_gate": nrm((DEPTH, D, SHARED_DIM), D ** -0.5),
        "sh_w_up": nrm((DEPTH, D, SHARED_DIM), D ** -0.5),
        "sh_w_down": nrm((DEPTH, SHARED_DIM, D), SHARED_DIM ** -0.5),
    }


def reference(x_prompt, x_sample, c_prompt, c_sample, ada_w, ada_b, norm_mix, norm_ffn,
              attn_w_in, attn_q_norm, attn_k_norm, attn_sink, attn_w_out,
              gla_w_in, gla_w_g2_f, gla_b_g_f, gla_w_g2_b, gla_b_g_b, gla_norm, gla_w_out,
              router_w, router_bias, exp_w_gate, exp_w_up, exp_w_down,
              sh_w_gate, sh_w_up, sh_w_down):
    def trunk(x, c):
        for layer in range(DEPTH):
            mod = jax.nn.silu(c) @ ada_w[layer] + ada_b[layer]
            sh_m, sc_m, g_m, sh_f, sc_f, g_f = jnp.split(mod[:, None, :], 6, axis=-1)
            h = rms_norm(x, norm_mix[layer]) * (1.0 + sc_m) + sh_m
            if layer % N_MIXERS == 0:
                a = layer // N_MIXERS
                mix = attention_mixer(h, attn_w_in[a], attn_q_norm[a], attn_k_norm[a],
                                      attn_sink[a], attn_w_out[a])
            else:
                g = layer // N_MIXERS
                mix = gla_mixer(h, gla_w_in[g], gla_w_g2_f[g], gla_b_g_f[g], gla_w_g2_b[g],
                                gla_b_g_b[g], gla_norm[g], gla_w_out[g])
            x = x + g_m * mix
            h = rms_norm(x, norm_ffn[layer]) * (1.0 + sc_f) + sh_f
            x = x + g_f * moe(h, router_w[layer], router_bias[layer], exp_w_gate[layer],
                              exp_w_up[layer], exp_w_down[layer], sh_w_gate[layer],
                              sh_w_up[layer], sh_w_down[layer])
        return x

    y_prompt = trunk(x_prompt, c_prompt)
    y_sample = trunk(x_sample, c_sample)
    return (y_prompt, y_sample)
```

```python
import functools

import numpy as np
import jax
import jax.numpy as jnp
from jax import lax
from jax.experimental import pallas as pl
from jax.experimental.pallas import tpu as pltpu

F32 = jnp.float32
BF16 = jnp.bfloat16
I32 = jnp.int32

D_MODEL = 2048
DEPTH = 2
EPS = 1e-6
HEAD_DIM = 128
N_Q_HEADS = 16
N_KV_HEADS = 4
Q_PER_KV = 4
WINDOW = 128
ATTN_BLOCK = 128
GLA_HEADS = 4
GLA_DK = 256
GLA_DV = 512
GLA_RANK = 16
GLA_TAU = 16.0
GLA_CHUNK = 64
GLA_MAIN = 2 * GLA_HEADS * GLA_DK + 2 * GLA_HEADS * GLA_DV
N_EXPERTS = 64
TOP_K = 8
N_GROUPS = 8
GROUP_SIZE = N_EXPERTS // N_GROUPS
TOPK_GROUPS = 4
EXPERT_DIM = 512
ROUTED_SCALE = 2.5
SLOT_BLOCK = 256

SUBLANES = 8
LANES = 128
TOKEN_TILE_ROWS = D_MODEL // (2 * LANES)
VMEM_LIMIT_BYTES = 56 * 1024 * 1024

NEG_INF = float("-inf")


def _params(semantics):
    return pltpu.CompilerParams(dimension_semantics=semantics, vmem_limit_bytes=VMEM_LIMIT_BYTES)


def _dot(a, b):
    return jnp.dot(a, b, preferred_element_type=F32)


def _dot_nt(a, b):
    return lax.dot_general(a, b, (((1,), (1,)), ((), ())), preferred_element_type=F32)


def _dot_tn(a, b):
    return lax.dot_general(a, b, (((0,), (0,)), ((), ())), preferred_element_type=F32)


def _split3(x):
    hi = x.astype(BF16)
    r1 = x - hi.astype(F32)
    mid = r1.astype(BF16)
    lo = (r1 - mid.astype(F32)).astype(BF16)
    return hi, mid, lo


def _sigmoid(x):
    return 1.0 / (1.0 + jnp.exp(-x))


def _silu(x):
    return x * _sigmoid(x)


def _pack_pair(lo, hi):
    lo_bits = lax.bitcast_convert_type(lo.astype(BF16).astype(F32), I32)
    hi_bits = lax.bitcast_convert_type(hi.astype(BF16).astype(F32), I32)
    return lax.shift_right_logical(lo_bits, 16) | (hi_bits & jnp.int32(-65536))


def _unpack_pair(word):
    lo = lax.bitcast_convert_type(lax.shift_left(word, 16), F32)
    hi = lax.bitcast_convert_type(word & jnp.int32(-65536), F32)
    return lo, hi


def _mod_kernel(c_ref, w_ref, b_ref, o_ref):
    a = _silu(c_ref[...])
    o_ref[0] = jnp.dot(a, w_ref[0], preferred_element_type=F32,
                       precision=lax.Precision.HIGHEST) + b_ref[0]


def _modulation(c_all, ada_w, ada_b):
    n = 6 * D_MODEL
    tn = 1024
    return pl.pallas_call(
        _mod_kernel,
        grid=(DEPTH, n // tn),
        in_specs=[
            pl.BlockSpec((SUBLANES, D_MODEL), lambda l, j: (0, 0)),
            pl.BlockSpec((1, D_MODEL, tn), lambda l, j: (l, 0, j)),
            pl.BlockSpec((1, 1, tn), lambda l, j: (l, 0, j)),
        ],
        out_specs=pl.BlockSpec((1, SUBLANES, tn), lambda l, j: (l, 0, j)),
        out_shape=jax.ShapeDtypeStruct((DEPTH, SUBLANES, n), F32),
        compiler_params=_params(("arbitrary", "arbitrary")),
        name="adaln_mod",
    )(c_all, ada_w, ada_b.reshape(DEPTH, 1, n))


def _normed(x_ref, g_ref, sc_ref, sh_ref):
    x = x_ref[...]
    ms = jnp.mean(x * x, axis=-1, keepdims=True)
    y = x * lax.rsqrt(ms + EPS) * g_ref[...]
    return y * (1.0 + sc_ref[0]) + sh_ref[0]


def _norm_kernel(x_ref, g_ref, sc_ref, sh_ref, h_ref):
    h_ref[...] = _normed(x_ref, g_ref, sc_ref, sh_ref).astype(BF16)


def _norm_router_kernel(x_ref, g_ref, sc_ref, sh_ref, wr_ref, h_ref, h3_ref, lg_ref):
    h = _normed(x_ref, g_ref, sc_ref, sh_ref)
    tm = h.shape[0]
    h_hi = h.astype(BF16)
    h_ref[...] = h_hi
    h_lo = (h - h_hi.astype(F32)).astype(BF16)
    w_hi, w_mid, _ = _split3(wr_ref[...])
    lg_ref[...] = _dot_nt(w_hi, h_hi) + _dot_nt(w_hi, h_lo) + _dot_nt(w_mid, h_hi)
    h_r = h_hi.astype(F32)
    for c in range(TOKEN_TILE_ROWS):
        base = 2 * LANES * c
        word = _pack_pair(h_r[:, base:base + LANES], h_r[:, base + LANES:base + 2 * LANES])
        h3_ref[pl.ds(c, tm, stride=TOKEN_TILE_ROWS), :] = word


def _mod_spec(slot, steps_per_seq, width=D_MODEL):
    return pl.BlockSpec((1, 1, width), lambda i: (6 * (i // steps_per_seq) + slot, 0, 0))


def _norm_mod(x, g, mod3, shift_slot, scale_slot, seq_len, router_wt=None):
    t = x.shape[0]
    tm = 512
    sps = seq_len // tm
    in_specs = [
        pl.BlockSpec((tm, D_MODEL), lambda i: (i, 0)),
        pl.BlockSpec((1, D_MODEL), lambda i: (0, 0)),
        _mod_spec(scale_slot, sps),
        _mod_spec(shift_slot, sps),
    ]
    h_spec = pl.BlockSpec((tm, D_MODEL), lambda i: (i, 0))
    h_shape = jax.ShapeDtypeStruct((t, D_MODEL), BF16)
    if router_wt is None:
        return pl.pallas_call(
            _norm_kernel, grid=(t // tm,), in_specs=in_specs, out_specs=h_spec, out_shape=h_shape,
            compiler_params=_params(("arbitrary",)), name="norm_mod",
        )(x, g, mod3, mod3)
    return pl.pallas_call(
        _norm_router_kernel,
        grid=(t // tm,),
        in_specs=in_specs + [pl.BlockSpec((N_EXPERTS, D_MODEL), lambda i: (0, 0))],
        out_specs=[
            h_spec,
            pl.BlockSpec((tm * TOKEN_TILE_ROWS, LANES), lambda i: (i, 0)),
            pl.BlockSpec((N_EXPERTS, tm), lambda i: (0, i)),
        ],
        out_shape=[
            h_shape,
            jax.ShapeDtypeStruct((t * TOKEN_TILE_ROWS, LANES), I32),
            jax.ShapeDtypeStruct((N_EXPERTS, t), F32),
        ],
        compiler_params=_params(("arbitrary",)),
        name="norm_mod_router",
    )(x, g, mod3, mod3, router_wt)


def _mm_kernel(a_ref, w_ref, o_ref):
    o_ref[...] = _dot(a_ref[...], w_ref[...]).astype(o_ref.dtype)


def _mm_res_kernel(a_ref, w_ref, x_ref, gate_ref, o_ref):
    o_ref[...] = x_ref[...] + gate_ref[0] * _dot(a_ref[...], w_ref[...])


def _matmul(a, w, out_dtype=BF16, tm=1024, tn=512):
    m, k = a.shape
    n = w.shape[1]
    return pl.pallas_call(
        _mm_kernel,
        grid=(m // tm, n // tn),
        in_specs=[pl.BlockSpec((tm, k), lambda i, j: (i, 0)),
                  pl.BlockSpec((k, tn), lambda i, j: (0, j))],
        out_specs=pl.BlockSpec((tm, tn), lambda i, j: (i, j)),
        out_shape=jax.ShapeDtypeStruct((m, n), out_dtype),
        compiler_params=_params(("arbitrary", "arbitrary")),
        name="matmul",
    )(a, w)


def _matmul_residual(a, w, x, mod3, gate_slot, seq_len, tm=1024, tn=512):
    m, k = a.shape
    n = w.shape[1]
    sps = seq_len // tm
    return pl.pallas_call(
        _mm_res_kernel,
        grid=(m // tm, n // tn),
        in_specs=[pl.BlockSpec((tm, k), lambda i, j: (i, 0)),
                  pl.BlockSpec((k, tn), lambda i, j: (0, j)),
                  pl.BlockSpec((tm, tn), lambda i, j: (i, j)),
                  pl.BlockSpec((1, 1, tn), lambda i, j: (6 * (i // sps) + gate_slot, 0, j))],
        out_specs=pl.BlockSpec((tm, tn), lambda i, j: (i, j)),
        out_shape=jax.ShapeDtypeStruct((m, n), F32),
        compiler_params=_params(("arbitrary", "arbitrary")),
        name="matmul_residual",
    )(a, w, x, mod3)


_ALIBI_SLOPES = [float(s) for s in
                 np.exp2(-8.0 * np.arange(1, N_Q_HEADS + 1, dtype=np.float32) / N_Q_HEADS)]


def _head_norm(x, gain):
    xf = x.astype(F32)
    ms = jnp.mean(xf * xf, axis=-1, keepdims=True)
    return xf * lax.rsqrt(ms + EPS) * gain


def _attn_kernel(sink_ref, q_ref, kp_ref, kc_ref, kn_ref, vp_ref, vc_ref, vn_ref,
                 qg_ref, kg_ref, o_ref, *, n_blocks):
    i = pl.program_id(1)
    span = ATTN_BLOCK + 2 * WINDOW
    t_idx = lax.broadcasted_iota(I32, (ATTN_BLOCK, span), 0)
    j_idx = lax.broadcasted_iota(I32, (ATTN_BLOCK, span), 1)
    dist = jnp.abs(t_idx - j_idx + WINDOW)
    valid = dist <= WINDOW
    valid = valid & ((j_idx >= WINDOW) | (i > 0))
    valid = valid & ((j_idx < WINDOW + ATTN_BLOCK) | (i < n_blocks - 1))
    dist_f = dist.astype(F32)
    q_gain = qg_ref[...] * (HEAD_DIM ** -0.5)
    k_gain = kg_ref[...]
    for g in range(N_KV_HEADS):
        cols = slice(g * HEAD_DIM, (g + 1) * HEAD_DIM)
        k_all = jnp.concatenate([kp_ref[:, cols], kc_ref[:, cols], kn_ref[:, cols]], axis=0)
        v_all = jnp.concatenate([vp_ref[:, cols], vc_ref[:, cols], vn_ref[:, cols]], axis=0)
        k_n = _head_norm(k_all, k_gain).astype(BF16)
        q_n = []
        for jh in range(Q_PER_KV):
            h = g * Q_PER_KV + jh
            q_n.append(_head_norm(q_ref[:, h * HEAD_DIM:(h + 1) * HEAD_DIM], q_gain).astype(BF16))
        s_all = _dot_nt(jnp.concatenate(q_n, axis=0), k_n)
        p_all, inv_den = [], []
        for jh in range(Q_PER_KV):
            h = g * Q_PER_KV + jh
            sink = sink_ref[h]
            s = s_all[jh * ATTN_BLOCK:(jh + 1) * ATTN_BLOCK] - _ALIBI_SLOPES[h] * dist_f
            s = jnp.where(valid, s, NEG_INF)
            m = jnp.maximum(jnp.max(s, axis=-1, keepdims=True), sink)
            p = jnp.exp(s - m)
            den = jnp.sum(p, axis=-1, keepdims=True) + jnp.exp(sink - m)
            p_all.append(p.astype(BF16))
            inv_den.append(1.0 / den)
        o_all = _dot(jnp.concatenate(p_all, axis=0), v_all)
        outs = [o_all[jh * ATTN_BLOCK:(jh + 1) * ATTN_BLOCK] * inv_den[jh] for jh in range(Q_PER_KV)]
        o_ref[:, g * Q_PER_KV * HEAD_DIM:(g + 1) * Q_PER_KV * HEAD_DIM] = (
            jnp.concatenate(outs, axis=1).astype(BF16))


def _attention(qkv, q_gain, k_gain, sink, batch, seq_len):
    t = qkv.shape[0]
    nb = seq_len // ATTN_BLOCK
    qw = N_Q_HEADS * HEAD_DIM
    kvw = N_KV_HEADS * HEAD_DIM
    k_col = qw // kvw
    v_col = k_col + 1

    def rows(b, i):
        return b * nb + i

    def kv_spec(col, shift):
        return pl.BlockSpec(
            (ATTN_BLOCK, kvw),
            lambda b, i: (rows(b, jnp.clip(i + shift, 0, nb - 1)), col))

    return pl.pallas_call(
        functools.partial(_attn_kernel, n_blocks=nb),
        grid=(batch, nb),
        in_specs=[
            pl.BlockSpec(memory_space=pltpu.SMEM),
            pl.BlockSpec((ATTN_BLOCK, qw), lambda b, i: (rows(b, i), 0)),
            kv_spec(k_col, -1), kv_spec(k_col, 0), kv_spec(k_col, 1),
            kv_spec(v_col, -1), kv_spec(v_col, 0), kv_spec(v_col, 1),
            pl.BlockSpec((1, HEAD_DIM), lambda b, i: (0, 0)),
            pl.BlockSpec((1, HEAD_DIM), lambda b, i: (0, 0)),
        ],
        out_specs=pl.BlockSpec((ATTN_BLOCK, qw), lambda b, i: (rows(b, i), 0)),
        out_shape=jax.ShapeDtypeStruct((t, qw), BF16),
        compiler_params=_params(("arbitrary", "arbitrary")),
        name="window_attention",
    )(sink, qkv, qkv, qkv, qkv, qkv, qkv, qkv, q_gain.reshape(1, HEAD_DIM), k_gain.reshape(1, HEAD_DIM))


GLA_GATE_PAD = LANES


def _gla_gate_kernel(h_ref, wlr_ref, wg_ref, bg_ref, laf_ref, lab_ref):
    h = h_ref[...]
    w_hi, w_mid, _ = _split3(wlr_ref[...])
    lr = _dot(h, w_hi) + _dot(h, w_mid)
    l_hi, l_mid, l_lo = _split3(lr)
    g_hi, g_mid, _ = _split3(wg_ref[...])
    z = (_dot(l_hi, g_hi) + _dot(l_mid, g_hi) + _dot(l_hi, g_mid) + _dot(l_lo, g_hi)
         + bg_ref[...])
    log_sig = jnp.minimum(z, 0.0) - jnp.log(1.0 + jnp.exp(-jnp.abs(z)))
    la = log_sig / GLA_TAU
    n = GLA_HEADS * GLA_DK
    laf_ref[...] = la[:, :n]
    lab_ref[...] = la[:, n:]


def _gla_gates(h, w_lr, w_g, b_g):
    t = h.shape[0]
    tm = 512
    n = GLA_HEADS * GLA_DK
    out = jax.ShapeDtypeStruct((t, n), F32)
    return pl.pallas_call(
        _gla_gate_kernel,
        grid=(t // tm,),
        in_specs=[pl.BlockSpec((tm, D_MODEL), lambda i: (i, 0)),
                  pl.BlockSpec((D_MODEL, GLA_GATE_PAD), lambda i: (0, 0)),
                  pl.BlockSpec((GLA_GATE_PAD, 2 * n), lambda i: (0, 0)),
                  pl.BlockSpec((1, 2 * n), lambda i: (0, 0))],
        out_specs=[pl.BlockSpec((tm, n), lambda i: (i, 0)), pl.BlockSpec((tm, n), lambda i: (i, 0))],
        out_shape=[out, out],
        compiler_params=_params(("arbitrary",)),
        name="gla_gates",
    )(h, w_lr, w_g, b_g)


GLA_STEP = 256


def _gla_chunks(q_ref, k_ref, v_ref, la_ref, st_ref, reverse):
    r_idx = lax.broadcasted_iota(I32, (GLA_CHUNK, GLA_CHUNK), 0)
    c_idx = lax.broadcasted_iota(I32, (GLA_CHUNK, GLA_CHUNK), 1)
    if reverse:
        sum_mat = (c_idx >= r_idx).astype(BF16)
        keep = c_idx > r_idx
    else:
        sum_mat = (c_idx <= r_idx).astype(BF16)
        keep = c_idx <= r_idx
    n_chunks = GLA_STEP // GLA_CHUNK
    order = range(n_chunks - 1, -1, -1) if reverse else range(n_chunks)
    outs = [None] * n_chunks
    for c in order:
        rows = slice(c * GLA_CHUNK, (c + 1) * GLA_CHUNK)
        la_hi, la_mid, la_lo = _split3(la_ref[rows, :])
        b = _dot(sum_mat, la_hi) + _dot(sum_mat, la_mid) + _dot(sum_mat, la_lo)
        edge = GLA_CHUNK - 1
        total = b[0:1, :] if reverse else b[edge:edge + 1, :]
        q = q_ref[rows, :].astype(F32) * (GLA_DK ** -0.5)
        k = k_ref[rows, :].astype(F32)
        v = v_ref[rows, :]
        q_dec = (q * jnp.exp(b)).astype(BF16)
        k_inv = (k * jnp.exp(-b)).astype(BF16)
        k_end = (k * jnp.exp(total - b)).astype(BF16)
        a = jnp.where(keep, _dot_nt(q_dec, k_inv), 0.0).astype(BF16)
        st = st_ref[...]
        outs[c] = _dot(a, v) + _dot_nt(q_dec, st.astype(BF16))
        st_ref[...] = st * jnp.exp(total) + _dot_tn(v, k_end)
    return jnp.concatenate(outs, axis=0)


def _gla_fwd_kernel(q_ref, k_ref, v_ref, la_ref, o_ref, st_ref):
    @pl.when(pl.program_id(2) == 0)
    def _():
        st_ref[...] = jnp.zeros_like(st_ref)
    o_ref[...] = _gla_chunks(q_ref, k_ref, v_ref, la_ref, st_ref, reverse=False)


def _gla_bwd_kernel(q_ref, k_ref, v_ref, la_ref, of_ref, r_ref, ng_ref, o_ref, st_ref):
    @pl.when(pl.program_id(2) == 0)
    def _():
        st_ref[...] = jnp.zeros_like(st_ref)
    o = of_ref[...] + _gla_chunks(q_ref, k_ref, v_ref, la_ref, st_ref, reverse=True)
    ms = jnp.mean(o * o, axis=-1, keepdims=True)
    o = o * lax.rsqrt(ms + EPS) * ng_ref[...]
    o_ref[...] = (o * _silu(r_ref[...].astype(F32))).astype(BF16)


def _gla(proj, la_f, la_b, norm_g, batch, seq_len):
    t = proj.shape[0]
    nblk = seq_len // GLA_STEP
    k_col = GLA_HEADS
    v_col = 2 * GLA_HEADS * GLA_DK // GLA_DV
    r_col = v_col + GLA_HEADS

    def specs(row):
        return [
            pl.BlockSpec((GLA_STEP, GLA_DK), lambda b, h, n: (row(b, n), h)),
            pl.BlockSpec((GLA_STEP, GLA_DK), lambda b, h, n: (row(b, n), k_col + h)),
            pl.BlockSpec((GLA_STEP, GLA_DV), lambda b, h, n: (row(b, n), v_col + h)),
            pl.BlockSpec((GLA_STEP, GLA_DK), lambda b, h, n: (row(b, n), h)),
        ]

    def out_spec(row):
        return pl.BlockSpec((GLA_STEP, GLA_DV), lambda b, h, n: (row(b, n), h))

    def fwd_row(b, n):
        return b * nblk + n

    def bwd_row(b, n):
        return b * nblk + (nblk - 1 - n)

    grid = (batch, GLA_HEADS, nblk)
    state = pltpu.VMEM((GLA_DV, GLA_DK), F32)
    sem = ("arbitrary", "arbitrary", "arbitrary")
    o_f = pl.pallas_call(
        _gla_fwd_kernel, grid=grid, in_specs=specs(fwd_row), out_specs=out_spec(fwd_row),
        out_shape=jax.ShapeDtypeStruct((t, GLA_HEADS * GLA_DV), F32),
        scratch_shapes=[state], compiler_params=_params(sem), name="gla_forward",
    )(proj, proj, proj, la_f)
    return pl.pallas_call(
        _gla_bwd_kernel, grid=grid,
        in_specs=specs(bwd_row) + [
            out_spec(bwd_row),
            pl.BlockSpec((GLA_STEP, GLA_DV), lambda b, h, n: (bwd_row(b, n), r_col + h)),
            pl.BlockSpec((1, GLA_DV), lambda b, h, n: (0, 0)),
        ],
        out_specs=out_spec(bwd_row),
        out_shape=jax.ShapeDtypeStruct((t, GLA_HEADS * GLA_DV), BF16),
        scratch_shapes=[state], compiler_params=_params(sem), name="gla_backward",
    )(proj, proj, proj, la_b, o_f, proj, norm_g.reshape(1, GLA_DV))


ROUTE_TILE = 512


def _route_kernel(lg_ref, bias_ref, idx_ref, w_ref, rank_ref, cnt_ref, run_ref, tri_ref):
    tm = ROUTE_TILE

    @pl.when(pl.program_id(0) == 0)
    def _():
        run_ref[...] = jnp.zeros_like(run_ref)
        r = lax.broadcasted_iota(I32, (tm, tm), 0)
        c = lax.broadcasted_iota(I32, (tm, tm), 1)
        tri_ref[...] = (r < c).astype(BF16)

    shape3 = (N_GROUPS, GROUP_SIZE, tm)
    scores = _sigmoid(lg_ref[...])
    biased = (scores + bias_ref[...]).reshape(shape3)
    scores3 = scores.reshape(shape3)
    in_grp = lax.broadcasted_iota(I32, shape3, 1)
    grp_id = lax.broadcasted_iota(I32, shape3, 0)
    exp_id = grp_id * GROUP_SIZE + in_grp

    m1 = jnp.max(biased, axis=1, keepdims=True)
    first = jnp.min(jnp.where(biased == m1, in_grp, GROUP_SIZE), axis=1, keepdims=True)
    m2 = jnp.max(jnp.where(in_grp == first, NEG_INF, biased), axis=1, keepdims=True)
    grp_score = m1 + m2

    g_iota = lax.broadcasted_iota(I32, (N_GROUPS, 1, tm), 0)
    g_sel = jnp.zeros((N_GROUPS, 1, tm), dtype=jnp.bool_)
    cur = grp_score
    for _ in range(TOPK_GROUPS):
        m = jnp.max(cur, axis=0, keepdims=True)
        pick = jnp.min(jnp.where(cur == m, g_iota, N_GROUPS), axis=0, keepdims=True)
        hit = g_iota == pick
        g_sel = g_sel | hit
        cur = jnp.where(hit, NEG_INF, cur)

    cand = jnp.where(g_sel, biased, NEG_INF)
    picks, weights = [], []
    sel = jnp.zeros(shape3, dtype=jnp.bool_)
    for _ in range(TOP_K):
        m = jnp.max(jnp.max(cand, axis=1, keepdims=True), axis=0, keepdims=True)
        pick = jnp.where(cand == m, exp_id, N_EXPERTS)
        pick = jnp.min(jnp.min(pick, axis=1, keepdims=True), axis=0, keepdims=True)
        hit = exp_id == pick
        sel = sel | hit
        cand = jnp.where(hit, NEG_INF, cand)
        w = jnp.where(hit, scores3, 0.0)
        weights.append(jnp.sum(jnp.sum(w, axis=1, keepdims=True), axis=0, keepdims=True))
        picks.append(pick)

    sel_f = jnp.where(sel, 1.0, 0.0).reshape(N_EXPERTS, tm)
    prefix = (_dot(sel_f.astype(BF16), tri_ref[...]) + run_ref[:, 0:1]).reshape(shape3)
    ranks = []
    for pick in picks:
        r = jnp.where(exp_id == pick, prefix, 0.0)
        ranks.append(jnp.sum(jnp.sum(r, axis=1, keepdims=True), axis=0, keepdims=True))
    new_run = run_ref[...] + jnp.sum(sel_f, axis=1, keepdims=True)
    run_ref[...] = new_run
    cnt_ref[...] = new_run.astype(I32)

    total = weights[0]
    for w in weights[1:]:
        total = total + w
    row = lax.broadcasted_iota(I32, (TOP_K, tm), 0)

    def stack(vals):
        out = jnp.zeros((TOP_K, tm), vals[0].dtype)
        for k, v in enumerate(vals):
            out = jnp.where(row == k, v.reshape(1, tm), out)
        return out

    idx_ref[...] = stack(picks)
    w_ref[...] = stack([w / total * ROUTED_SCALE for w in weights])
    rank_ref[...] = stack(ranks).astype(I32)


def _route(logits_t, bias):
    t = logits_t.shape[1]
    tm = ROUTE_TILE
    tok = pl.BlockSpec((TOP_K, tm), lambda i: (0, i))
    idx, w, rank, cnt = pl.pallas_call(
        _route_kernel,
        grid=(t // tm,),
        in_specs=[pl.BlockSpec((N_EXPERTS, tm), lambda i: (0, i)),
                  pl.BlockSpec((N_EXPERTS, 1), lambda i: (0, 0))],
        out_specs=[tok, tok, tok, pl.BlockSpec((N_EXPERTS, LANES), lambda i: (0, 0))],
        out_shape=[jax.ShapeDtypeStruct((TOP_K, t), I32),
                   jax.ShapeDtypeStruct((TOP_K, t), F32),
                   jax.ShapeDtypeStruct((TOP_K, t), I32),
                   jax.ShapeDtypeStruct((N_EXPERTS, LANES), I32)],
        scratch_shapes=[pltpu.VMEM((N_EXPERTS, LANES), F32), pltpu.VMEM((tm, tm), BF16)],
        compiler_params=_params(("arbitrary",)),
        name="moe_route",
    )(logits_t, bias.reshape(N_EXPERTS, 1))
    return idx, w, rank, cnt[:, 0]


DISPATCH_TILE = 512


def _token_rows(i):
    return pl.ds(pl.multiple_of(i * TOKEN_TILE_ROWS, TOKEN_TILE_ROWS), TOKEN_TILE_ROWS)


def _block_rows(b):
    n = SLOT_BLOCK * TOKEN_TILE_ROWS
    return pl.ds(pl.multiple_of(b * n, n), n)


def _dispatch_kernel(cnt_ref, start_ref, nu_ref, pos_ref, h3_ref, x3_ref, zero_ref, sem, pad_sem,
                     *, n_blocks):
    i = pl.program_id(0)
    tm = DISPATCH_TILE
    base = i * tm

    def issue(t, carry):
        for k in range(TOP_K):
            pltpu.make_async_copy(h3_ref.at[_token_rows(base + t)],
                                  x3_ref.at[_token_rows(pos_ref[k, t])], sem).start()
        return carry

    lax.fori_loop(0, tm, issue, 0)

    @pl.when(i == 0)
    def _():
        zero_ref[...] = jnp.zeros_like(zero_ref)
        zero_tile = zero_ref.at[pl.ds(0, TOKEN_TILE_ROWS)]

        def per_expert(e, n_pad):
            cnt = cnt_ref[e]
            padded = (cnt + SLOT_BLOCK - 1) // SLOT_BLOCK * SLOT_BLOCK

            def pad_row(r, c):
                pltpu.make_async_copy(zero_tile, x3_ref.at[_token_rows(start_ref[e] + r)],
                                      pad_sem).start()
                return c

            lax.fori_loop(cnt, padded, pad_row, 0)
            return n_pad + (padded - cnt)

        n_pad = lax.fori_loop(0, N_EXPERTS, per_expert, 0)

        def tail_block(b, c):
            pltpu.make_async_copy(zero_ref, x3_ref.at[_block_rows(b)], pad_sem).start()
            return c

        lax.fori_loop(nu_ref[0], n_blocks, tail_block, 0)

        def pad_wait(r, c):
            pltpu.make_async_copy(zero_tile, x3_ref.at[_token_rows(0)], pad_sem).wait()
            return c

        lax.fori_loop(0, n_pad, pad_wait, 0)

        def tail_wait(b, c):
            pltpu.make_async_copy(zero_ref, x3_ref.at[_block_rows(0)], pad_sem).wait()
            return c

        lax.fori_loop(nu_ref[0], n_blocks, tail_wait, 0)

    n_rows = tm * TOP_K * TOKEN_TILE_ROWS
    pltpu.make_async_copy(x3_ref.at[pl.ds(0, n_rows)], x3_ref.at[pl.ds(0, n_rows)], sem).wait()


def _dispatch(h3, pos, counts, starts, n_used, n_blocks):
    t = pos.shape[1]
    tm = DISPATCH_TILE
    return pl.pallas_call(
        functools.partial(_dispatch_kernel, n_blocks=n_blocks),
        grid_spec=pltpu.PrefetchScalarGridSpec(
            num_scalar_prefetch=3,
            grid=(t // tm,),
            in_specs=[pl.BlockSpec((TOP_K, tm), lambda i, c, s, u: (0, i), memory_space=pltpu.SMEM),
                      pl.BlockSpec(memory_space=pl.ANY)],
            out_specs=pl.BlockSpec(memory_space=pl.ANY),
            scratch_shapes=[pltpu.VMEM((SLOT_BLOCK * TOKEN_TILE_ROWS, LANES), I32),
                            pltpu.SemaphoreType.DMA, pltpu.SemaphoreType.DMA],
        ),
        out_shape=jax.ShapeDtypeStruct((n_blocks * SLOT_BLOCK * TOKEN_TILE_ROWS, LANES), I32),
        compiler_params=_params(("arbitrary",)),
        name="moe_dispatch",
    )(counts, starts, n_used, pos, h3)


def _expert_kernel(be_ref, nu_ref, x3_ref, wg_ref, wu_ref, wd_ref, y3_ref):
    @pl.when(pl.program_id(0) >= nu_ref[0])
    def _():
        y3_ref[...] = jnp.zeros_like(y3_ref)

    @pl.when(pl.program_id(0) < nu_ref[0])
    def _():
        m = SLOT_BLOCK
        pieces = []
        for c in range(TOKEN_TILE_ROWS):
            lo, hi = _unpack_pair(x3_ref[pl.ds(c, m, stride=TOKEN_TILE_ROWS), :])
            pieces += [lo.astype(BF16), hi.astype(BF16)]
        x = jnp.concatenate(pieces, axis=1)
        a = (_silu(_dot(x, wg_ref[0])) * _dot(x, wu_ref[0])).astype(BF16)
        y = _dot(a, wd_ref[0])
        for c in range(TOKEN_TILE_ROWS):
            base = 2 * LANES * c
            y3_ref[pl.ds(c, m, stride=TOKEN_TILE_ROWS), :] = _pack_pair(
                y[:, base:base + LANES], y[:, base + LANES:base + 2 * LANES])


def _experts(x3, block_expert, n_used, w_gate, w_up, w_down):
    n_blocks = block_expert.shape[0]
    rows = SLOT_BLOCK * TOKEN_TILE_ROWS

    def blk(b, be, nu):
        return jnp.minimum(b, nu[0] - 1)

    return pl.pallas_call(
        _expert_kernel,
        grid_spec=pltpu.PrefetchScalarGridSpec(
            num_scalar_prefetch=2,
            grid=(n_blocks,),
            in_specs=[
                pl.BlockSpec((rows, LANES), lambda b, be, nu: (blk(b, be, nu), 0)),
                pl.BlockSpec((1, D_MODEL, EXPERT_DIM), lambda b, be, nu: (be[blk(b, be, nu)], 0, 0)),
                pl.BlockSpec((1, D_MODEL, EXPERT_DIM), lambda b, be, nu: (be[blk(b, be, nu)], 0, 0)),
                pl.BlockSpec((1, EXPERT_DIM, D_MODEL), lambda b, be, nu: (be[blk(b, be, nu)], 0, 0)),
            ],
            out_specs=pl.BlockSpec((rows, LANES), lambda b, be, nu: (b, 0)),
        ),
        out_shape=jax.ShapeDtypeStruct(x3.shape, I32),
        compiler_params=_params(("arbitrary",)),
        name="moe_experts",
    )(block_expert, n_used, x3, w_gate, w_up, w_down)


COMBINE_TILE = 256


def _combine_kernel(pos_ref, x_ref, h_ref, w_ref, gate_ref, sg_ref, su_ref, sd_ref, y3_ref,
                    o_ref, buf_ref, sem):
    tm = COMBINE_TILE

    def issue(t, carry):
        for k in range(TOP_K):
            pltpu.make_async_copy(y3_ref.at[_token_rows(pos_ref[k, t])],
                                  buf_ref.at[_token_rows(k * tm + t)], sem).start()
        return carry

    lax.fori_loop(0, tm, issue, 0)

    h = h_ref[...]
    a = (_silu(_dot(h, sg_ref[...])) * _dot(h, su_ref[...])).astype(BF16)
    shared = _dot(a, sd_ref[...])

    pltpu.make_async_copy(y3_ref.at[pl.ds(0, buf_ref.shape[0])], buf_ref, sem).wait()

    gate = gate_ref[0]
    wk = [w_ref[:, k:k + 1] for k in range(TOP_K)]
    for c in range(TOKEN_TILE_ROWS):
        lo_acc = None
        for k in range(TOP_K):
            word = buf_ref[pl.ds(k * tm * TOKEN_TILE_ROWS + c, tm, stride=TOKEN_TILE_ROWS), :]
            lo, hi = _unpack_pair(word)
            if lo_acc is None:
                lo_acc, hi_acc = wk[k] * lo, wk[k] * hi
            else:
                lo_acc, hi_acc = lo_acc + wk[k] * lo, hi_acc + wk[k] * hi
        base = 2 * LANES * c
        for off, acc in ((base, lo_acc), (base + LANES, hi_acc)):
            cols = slice(off, off + LANES)
            o_ref[:, cols] = x_ref[:, cols] + gate[:, cols] * (acc + shared[:, cols])


def _combine(x, h, y3, pos, w_tok, mod3, gate_slot, seq_len, s_gate, s_up, s_down):
    t = x.shape[0]
    tm = COMBINE_TILE
    sps = seq_len // tm
    full = lambda shape: pl.BlockSpec(shape, lambda i: tuple(0 for _ in shape))
    return pl.pallas_call(
        _combine_kernel,
        grid=(t // tm,),
        in_specs=[
            pl.BlockSpec((TOP_K, tm), lambda i: (0, i), memory_space=pltpu.SMEM),
            pl.BlockSpec((tm, D_MODEL), lambda i: (i, 0)),
            pl.BlockSpec((tm, D_MODEL), lambda i: (i, 0)),
            pl.BlockSpec((tm, TOP_K), lambda i: (i, 0)),
            _mod_spec(gate_slot, sps),
            full((D_MODEL, EXPERT_DIM)), full((D_MODEL, EXPERT_DIM)), full((EXPERT_DIM, D_MODEL)),
            pl.BlockSpec(memory_space=pl.ANY),
        ],
        out_specs=pl.BlockSpec((tm, D_MODEL), lambda i: (i, 0)),
        out_shape=jax.ShapeDtypeStruct((t, D_MODEL), F32),
        scratch_shapes=[pltpu.VMEM((TOP_K * tm * TOKEN_TILE_ROWS, LANES), I32),
                        pltpu.SemaphoreType.DMA],
        compiler_params=_params(("arbitrary",)),
        name="moe_combine",
    )(pos, x, h, w_tok, mod3, s_gate, s_up, s_down, y3)


def _moe_layer(x, mod3, norm_g, seq_len, router_wt, router_bias, w_gate, w_up, w_down,
               s_gate, s_up, s_down):
    t = x.shape[0]
    h, h3, logits_t = _norm_mod(x, norm_g, mod3, 3, 4, seq_len, router_wt)
    idx, w, rank, counts = _route(logits_t, router_bias)
    n_blocks = -(-(t * TOP_K + N_EXPERTS * (SLOT_BLOCK - 1)) // SLOT_BLOCK)
    padded = (counts + SLOT_BLOCK - 1) // SLOT_BLOCK * SLOT_BLOCK
    padded_end = jnp.cumsum(padded)
    starts = (padded_end - padded).astype(I32)
    block_expert = jnp.minimum(
        jnp.searchsorted(padded_end, jnp.arange(n_blocks, dtype=I32) * SLOT_BLOCK, side="right"),
        N_EXPERTS - 1).astype(I32)
    n_used = (padded_end[-1:] // SLOT_BLOCK).astype(I32)
    pos = starts[idx] + rank
    x3 = _dispatch(h3, pos, counts.astype(I32), starts, n_used, n_blocks)
    y3 = _experts(x3, block_expert, n_used, w_gate, w_up, w_down)
    return _combine(x, h, y3, pos, w.T, mod3, 5, seq_len, s_gate, s_up, s_down)


def _trunk(x, mod, batch, seq_len, p):
    t = batch * seq_len
    x = x.reshape(t, D_MODEL)
    for layer in range(DEPTH):
        mod3 = mod[layer].reshape(batch * 6, 1, D_MODEL)
        g_mix = p["norm_mix"][layer].reshape(1, D_MODEL)
        g_ffn = p["norm_ffn"][layer].reshape(1, D_MODEL)
        h = _norm_mod(x, g_mix, mod3, 0, 1, seq_len)
        if layer % 2 == 0:
            a = layer // 2
            qkv = _matmul(h, p["attn_w_in"][a])
            o = _attention(qkv, p["attn_q_norm"][a], p["attn_k_norm"][a], p["attn_sink"][a],
                           batch, seq_len)
            x = _matmul_residual(o, p["attn_w_out"][a], x, mod3, 2, seq_len)
        else:
            g = layer // 2
            proj = _matmul(h, p["gla_w_main"][g])
            la_f, la_b = _gla_gates(h, p["gla_w_lr"][g], p["gla_w_g"][g], p["gla_b_g"][g])
            o = _gla(proj, la_f, la_b, p["gla_norm"][g], batch, seq_len)
            x = _matmul_residual(o, p["gla_w_out"][g], x, mod3, 2, seq_len)
        x = _moe_layer(x, mod3, g_ffn, seq_len, p["router_wt"][layer], p["router_bias"][layer],
                       p["exp_w_gate"][layer], p["exp_w_up"][layer], p["exp_w_down"][layer],
                       p["sh_w_gate"][layer], p["sh_w_up"][layer], p["sh_w_down"][layer])
    return x.reshape(batch, seq_len, D_MODEL)


def kernel(x_prompt, x_sample, c_prompt, c_sample, ada_w, ada_b, norm_mix, norm_ffn, attn_w_in, attn_q_norm, attn_k_norm, attn_sink, attn_w_out, gla_w_in, gla_w_g2_f, gla_b_g_f, gla_w_g2_b, gla_b_g_b, gla_norm, gla_w_out, router_w, router_bias, exp_w_gate, exp_w_up, exp_w_down, sh_w_gate, sh_w_up, sh_w_down):
    bp, sp, _ = x_prompt.shape
    bs, ss, _ = x_sample.shape
    n_gate = GLA_HEADS * GLA_DK
    n_layers_gla = gla_w_in.shape[0]

    w_lr = jnp.zeros((n_layers_gla, D_MODEL, GLA_GATE_PAD), F32)
    w_lr = w_lr.at[:, :, :2 * GLA_RANK].set(gla_w_in[:, :, GLA_MAIN:])
    w_g = jnp.zeros((n_layers_gla, GLA_GATE_PAD, 2 * n_gate), F32)
    w_g = w_g.at[:, :GLA_RANK, :n_gate].set(gla_w_g2_f)
    w_g = w_g.at[:, GLA_RANK:2 * GLA_RANK, n_gate:].set(gla_w_g2_b)
    p = {
        "norm_mix": norm_mix, "norm_ffn": norm_ffn,
        "attn_w_in": attn_w_in.astype(BF16), "attn_q_norm": attn_q_norm,
        "attn_k_norm": attn_k_norm, "attn_sink": attn_sink, "attn_w_out": attn_w_out.astype(BF16),
        "gla_w_main": gla_w_in[:, :, :GLA_MAIN].astype(BF16), "gla_w_lr": w_lr, "gla_w_g": w_g,
        "gla_b_g": jnp.concatenate([gla_b_g_f, gla_b_g_b], axis=-1).reshape(n_layers_gla, 1, 2 * n_gate),
        "gla_norm": gla_norm, "gla_w_out": gla_w_out.astype(BF16),
        "router_wt": jnp.swapaxes(router_w, 1, 2), "router_bias": router_bias,
        "exp_w_gate": exp_w_gate.astype(BF16), "exp_w_up": exp_w_up.astype(BF16),
        "exp_w_down": exp_w_down.astype(BF16),
        "sh_w_gate": sh_w_gate.astype(BF16), "sh_w_up": sh_w_up.astype(BF16),
        "sh_w_down": sh_w_down.astype(BF16),
    }

    c_all = jnp.zeros((SUBLANES, D_MODEL), F32).at[:bp + bs].set(jnp.concatenate([c_prompt, c_sample], 0))
    mod = _modulation(c_all, ada_w, ada_b)
    y_prompt = _trunk(x_prompt, mod[:, :bp], bp, sp, p)
    y_sample = _trunk(x_sample, mod[:, bp:bp + bs], bs, ss, p)
    return (y_prompt, y_sample)
```

```python
import functools

import numpy as np
import jax
import jax.numpy as jnp
from jax import lax
from jax.experimental import pallas as pl
from jax.experimental.pallas import tpu as pltpu

F32 = jnp.float32
BF16 = jnp.bfloat16
I32 = jnp.int32

D_MODEL = 2048
DEPTH = 2
EPS = 1e-6
HEAD_DIM = 128
N_Q_HEADS = 16
N_KV_HEADS = 4
Q_PER_KV = 4
WINDOW = 128
ATTN_BLOCK = 128
GLA_HEADS = 4
GLA_DK = 256
GLA_DV = 512
GLA_RANK = 16
GLA_TAU = 16.0
GLA_CHUNK = 64
GLA_MAIN = 2 * GLA_HEADS * GLA_DK + 2 * GLA_HEADS * GLA_DV
N_EXPERTS = 64
TOP_K = 8
N_GROUPS = 8
GROUP_SIZE = N_EXPERTS // N_GROUPS
TOPK_GROUPS = 4
EXPERT_DIM = 512
ROUTED_SCALE = 2.5
SLOT_BLOCK = 256

SUBLANES = 8
LANES = 128
TOKEN_TILE_ROWS = D_MODEL // (2 * LANES)
VMEM_LIMIT_BYTES = 56 * 1024 * 1024

NEG_INF = float("-inf")


def _params(semantics):
    return pltpu.CompilerParams(dimension_semantics=semantics, vmem_limit_bytes=VMEM_LIMIT_BYTES)


def _dot(a, b):
    return jnp.dot(a, b, preferred_element_type=F32)


def _dot_nt(a, b):
    return lax.dot_general(a, b, (((1,), (1,)), ((), ())), preferred_element_type=F32)


def _dot_tn(a, b):
    return lax.dot_general(a, b, (((0,), (0,)), ((), ())), preferred_element_type=F32)


def _split3(x):
    hi = x.astype(BF16)
    r1 = x - hi.astype(F32)
    mid = r1.astype(BF16)
    lo = (r1 - mid.astype(F32)).astype(BF16)
    return hi, mid, lo


def _sigmoid(x):
    return 1.0 / (1.0 + jnp.exp(-x))


def _silu(x):
    return x * _sigmoid(x)


def _pack_pair(lo, hi):
    lo_bits = lax.bitcast_convert_type(lo.astype(BF16).astype(F32), I32)
    hi_bits = lax.bitcast_convert_type(hi.astype(BF16).astype(F32), I32)
    return lax.shift_right_logical(lo_bits, 16) | (hi_bits & jnp.int32(-65536))


def _unpack_pair(word):
    lo = lax.bitcast_convert_type(lax.shift_left(word, 16), F32)
    hi = lax.bitcast_convert_type(word & jnp.int32(-65536), F32)
    return lo, hi


def _mod_kernel(c_ref, w_ref, b_ref, o_ref):
    a = _silu(c_ref[...])
    o_ref[0] = jnp.dot(a, w_ref[0], preferred_element_type=F32,
                       precision=lax.Precision.HIGHEST) + b_ref[0]


def _modulation(c_all, ada_w, ada_b):
    n = 6 * D_MODEL
    tn = 1024
    return pl.pallas_call(
        _mod_kernel,
        grid=(DEPTH, n // tn),
        in_specs=[
            pl.BlockSpec((SUBLANES, D_MODEL), lambda l, j: (0, 0)),
            pl.BlockSpec((1, D_MODEL, tn), lambda l, j: (l, 0, j)),
            pl.BlockSpec((1, 1, tn), lambda l, j: (l, 0, j)),
        ],
        out_specs=pl.BlockSpec((1, SUBLANES, tn), lambda l, j: (l, 0, j)),
        out_shape=jax.ShapeDtypeStruct((DEPTH, SUBLANES, n), F32),
        compiler_params=_params(("arbitrary", "arbitrary")),
        name="adaln_mod",
    )(c_all, ada_w, ada_b.reshape(DEPTH, 1, n))


def _normed(x_ref, g_ref, sc_ref, sh_ref):
    x = x_ref[...]
    ms = jnp.mean(x * x, axis=-1, keepdims=True)
    y = x * lax.rsqrt(ms + EPS) * g_ref[...]
    return y * (1.0 + sc_ref[0]) + sh_ref[0]


def _norm_kernel(x_ref, g_ref, sc_ref, sh_ref, h_ref):
    h_ref[...] = _normed(x_ref, g_ref, sc_ref, sh_ref).astype(BF16)


def _norm_router_kernel(x_ref, g_ref, sc_ref, sh_ref, wr_ref, h_ref, h3_ref, lg_ref):
    h = _normed(x_ref, g_ref, sc_ref, sh_ref)
    tm = h.shape[0]
    h_hi = h.astype(BF16)
    h_ref[...] = h_hi
    h_lo = (h - h_hi.astype(F32)).astype(BF16)
    w_hi, w_mid, _ = _split3(wr_ref[...])
    lg_ref[...] = _dot_nt(w_hi, h_hi) + _dot_nt(w_hi, h_lo) + _dot_nt(w_mid, h_hi)
    h_r = h_hi.astype(F32)
    for c in range(TOKEN_TILE_ROWS):
        base = 2 * LANES * c
        word = _pack_pair(h_r[:, base:base + LANES], h_r[:, base + LANES:base + 2 * LANES])
        h3_ref[pl.ds(c, tm, stride=TOKEN_TILE_ROWS), :] = word


def _mod_spec(slot, steps_per_seq, width=D_MODEL):
    return pl.BlockSpec((1, 1, width), lambda i: (6 * (i // steps_per_seq) + slot, 0, 0))


def _norm_mod(x, g, mod3, shift_slot, scale_slot, seq_len, router_wt=None):
    t = x.shape[0]
    tm = 512
    sps = seq_len // tm
    in_specs = [
        pl.BlockSpec((tm, D_MODEL), lambda i: (i, 0)),
        pl.BlockSpec((1, D_MODEL), lambda i: (0, 0)),
        _mod_spec(scale_slot, sps),
        _mod_spec(shift_slot, sps),
    ]
    h_spec = pl.BlockSpec((tm, D_MODEL), lambda i: (i, 0))
    h_shape = jax.ShapeDtypeStruct((t, D_MODEL), BF16)
    if router_wt is None:
        return pl.pallas_call(
            _norm_kernel, grid=(t // tm,), in_specs=in_specs, out_specs=h_spec, out_shape=h_shape,
            compiler_params=_params(("arbitrary",)), name="norm_mod",
        )(x, g, mod3, mod3)
    return pl.pallas_call(
        _norm_router_kernel,
        grid=(t // tm,),
        in_specs=in_specs + [pl.BlockSpec((N_EXPERTS, D_MODEL), lambda i: (0, 0))],
        out_specs=[
            h_spec,
            pl.BlockSpec((tm * TOKEN_TILE_ROWS, LANES), lambda i: (i, 0)),
            pl.BlockSpec((N_EXPERTS, tm), lambda i: (0, i)),
        ],
        out_shape=[
            h_shape,
            jax.ShapeDtypeStruct((t * TOKEN_TILE_ROWS, LANES), I32),
            jax.ShapeDtypeStruct((N_EXPERTS, t), F32),
        ],
        compiler_params=_params(("arbitrary",)),
        name="norm_mod_router",
    )(x, g, mod3, mod3, router_wt)


def _mm_kernel(a_ref, w_ref, o_ref):
    o_ref[...] = _dot(a_ref[...], w_ref[...]).astype(o_ref.dtype)


def _mm_res_kernel(a_ref, w_ref, x_ref, gate_ref, o_ref):
    o_ref[...] = x_ref[...] + gate_ref[0] * _dot(a_ref[...], w_ref[...])


def _matmul(a, w, out_dtype=BF16, tm=1024, tn=512):
    m, k = a.shape
    n = w.shape[1]
    return pl.pallas_call(
        _mm_kernel,
        grid=(m // tm, n // tn),
        in_specs=[pl.BlockSpec((tm, k), lambda i, j: (i, 0)),
                  pl.BlockSpec((k, tn), lambda i, j: (0, j))],
        out_specs=pl.BlockSpec((tm, tn), lambda i, j: (i, j)),
        out_shape=jax.ShapeDtypeStruct((m, n), out_dtype),
        compiler_params=_params(("arbitrary", "arbitrary")),
        name="matmul",
    )(a, w)


def _matmul_residual(a, w, x, mod3, gate_slot, seq_len, tm=1024, tn=512):
    m, k = a.shape
    n = w.shape[1]
    sps = seq_len // tm
    return pl.pallas_call(
        _mm_res_kernel,
        grid=(m // tm, n // tn),
        in_specs=[pl.BlockSpec((tm, k), lambda i, j: (i, 0)),
                  pl.BlockSpec((k, tn), lambda i, j: (0, j)),
                  pl.BlockSpec((tm, tn), lambda i, j: (i, j)),
                  pl.BlockSpec((1, 1, tn), lambda i, j: (6 * (i // sps) + gate_slot, 0, j))],
        out_specs=pl.BlockSpec((tm, tn), lambda i, j: (i, j)),
        out_shape=jax.ShapeDtypeStruct((m, n), F32),
        compiler_params=_params(("arbitrary", "arbitrary")),
        name="matmul_residual",
    )(a, w, x, mod3)


_ALIBI_SLOPES = [float(s) for s in
                 np.exp2(-8.0 * np.arange(1, N_Q_HEADS + 1, dtype=np.float32) / N_Q_HEADS)]


def _head_norm(x, gain):
    xf = x.astype(F32)
    ms = jnp.mean(xf * xf, axis=-1, keepdims=True)
    return xf * lax.rsqrt(ms + EPS) * gain


def _attn_kernel(sink_ref, q_ref, kp_ref, kc_ref, kn_ref, vp_ref, vc_ref, vn_ref,
                 qg_ref, kg_ref, o_ref, *, n_blocks):
    i = pl.program_id(1)
    span = ATTN_BLOCK + 2 * WINDOW
    t_idx = lax.broadcasted_iota(I32, (ATTN_BLOCK, span), 0)
    j_idx = lax.broadcasted_iota(I32, (ATTN_BLOCK, span), 1)
    dist = jnp.abs(t_idx - j_idx + WINDOW)
    valid = dist <= WINDOW
    valid = valid & ((j_idx >= WINDOW) | (i > 0))
    valid = valid & ((j_idx < WINDOW + ATTN_BLOCK) | (i < n_blocks - 1))
    dist_f = dist.astype(F32)
    q_gain = qg_ref[...] * (HEAD_DIM ** -0.5)
    k_gain = kg_ref[...]
    for g in range(N_KV_HEADS):
        cols = slice(g * HEAD_DIM, (g + 1) * HEAD_DIM)
        k_all = jnp.concatenate([kp_ref[:, cols], kc_ref[:, cols], kn_ref[:, cols]], axis=0)
        v_all = jnp.concatenate([vp_ref[:, cols], vc_ref[:, cols], vn_ref[:, cols]], axis=0)
        k_n = _head_norm(k_all, k_gain).astype(BF16)
        q_n = []
        for jh in range(Q_PER_KV):
            h = g * Q_PER_KV + jh
            q_n.append(_head_norm(q_ref[:, h * HEAD_DIM:(h + 1) * HEAD_DIM], q_gain).astype(BF16))
        s_all = _dot_nt(jnp.concatenate(q_n, axis=0), k_n)
        p_all, inv_den = [], []
        for jh in range(Q_PER_KV):
            h = g * Q_PER_KV + jh
            sink = sink_ref[h]
            s = s_all[jh * ATTN_BLOCK:(jh + 1) * ATTN_BLOCK] - _ALIBI_SLOPES[h] * dist_f
            s = jnp.where(valid, s, NEG_INF)
            m = jnp.maximum(jnp.max(s, axis=-1, keepdims=True), sink)
            p = jnp.exp(s - m)
            den = jnp.sum(p, axis=-1, keepdims=True) + jnp.exp(sink - m)
            p_all.append(p.astype(BF16))
            inv_den.append(1.0 / den)
        o_all = _dot(jnp.concatenate(p_all, axis=0), v_all)
        outs = [o_all[jh * ATTN_BLOCK:(jh + 1) * ATTN_BLOCK] * inv_den[jh] for jh in range(Q_PER_KV)]
        o_ref[:, g * Q_PER_KV * HEAD_DIM:(g + 1) * Q_PER_KV * HEAD_DIM] = (
            jnp.concatenate(outs, axis=1).astype(BF16))


def _attention(qkv, q_gain, k_gain, sink, batch, seq_len):
    t = qkv.shape[0]
    nb = seq_len // ATTN_BLOCK
    qw = N_Q_HEADS * HEAD_DIM
    kvw = N_KV_HEADS * HEAD_DIM
    k_col = qw // kvw
    v_col = k_col + 1

    def rows(b, i):
        return b * nb + i

    def kv_spec(col, shift):
        return pl.BlockSpec(
            (ATTN_BLOCK, kvw),
            lambda b, i: (rows(b, jnp.clip(i + shift, 0, nb - 1)), col))

    return pl.pallas_call(
        functools.partial(_attn_kernel, n_blocks=nb),
        grid=(batch, nb),
        in_specs=[
            pl.BlockSpec(memory_space=pltpu.SMEM),
            pl.BlockSpec((ATTN_BLOCK, qw), lambda b, i: (rows(b, i), 0)),
            kv_spec(k_col, -1), kv_spec(k_col, 0), kv_spec(k_col, 1),
            kv_spec(v_col, -1), kv_spec(v_col, 0), kv_spec(v_col, 1),
            pl.BlockSpec((1, HEAD_DIM), lambda b, i: (0, 0)),
            pl.BlockSpec((1, HEAD_DIM), lambda b, i: (0, 0)),
        ],
        out_specs=pl.BlockSpec((ATTN_BLOCK, qw), lambda b, i: (rows(b, i), 0)),
        out_shape=jax.ShapeDtypeStruct((t, qw), BF16),
        compiler_params=_params(("arbitrary", "arbitrary")),
        name="window_attention",
    )(sink, qkv, qkv, qkv, qkv, qkv, qkv, qkv, q_gain.reshape(1, HEAD_DIM), k_gain.reshape(1, HEAD_DIM))


GLA_GATE_PAD = LANES


def _gla_gate_kernel(h_ref, wlr_ref, wg_ref, bg_ref, laf_ref, lab_ref):
    h = h_ref[...]
    w_hi, w_mid, _ = _split3(wlr_ref[...])
    lr = _dot(h, w_hi) + _dot(h, w_mid)
    l_hi, l_mid, l_lo = _split3(lr)
    g_hi, g_mid, _ = _split3(wg_ref[...])
    z = (_dot(l_hi, g_hi) + _dot(l_mid, g_hi) + _dot(l_hi, g_mid) + _dot(l_lo, g_hi)
         + bg_ref[...])
    log_sig = jnp.minimum(z, 0.0) - jnp.log(1.0 + jnp.exp(-jnp.abs(z)))
    la = log_sig / GLA_TAU
    n = GLA_HEADS * GLA_DK
    laf_ref[...] = la[:, :n]
    lab_ref[...] = la[:, n:]


def _gla_gates(h, w_lr, w_g, b_g):
    t = h.shape[0]
    tm = 512
    n = GLA_HEADS * GLA_DK
    out = jax.ShapeDtypeStruct((t, n), F32)
    return pl.pallas_call(
        _gla_gate_kernel,
        grid=(t // tm,),
        in_specs=[pl.BlockSpec((tm, D_MODEL), lambda i: (i, 0)),
                  pl.BlockSpec((D_MODEL, GLA_GATE_PAD), lambda i: (0, 0)),
                  pl.BlockSpec((GLA_GATE_PAD, 2 * n), lambda i: (0, 0)),
                  pl.BlockSpec((1, 2 * n), lambda i: (0, 0))],
        out_specs=[pl.BlockSpec((tm, n), lambda i: (i, 0)), pl.BlockSpec((tm, n), lambda i: (i, 0))],
        out_shape=[out, out],
        compiler_params=_params(("arbitrary",)),
        name="gla_gates",
    )(h, w_lr, w_g, b_g)


GLA_STEP = 256


def _gla_chunks(q_ref, k_ref, v_ref, la_ref, st_ref, reverse):
    r_idx = lax.broadcasted_iota(I32, (GLA_CHUNK, GLA_CHUNK), 0)
    c_idx = lax.broadcasted_iota(I32, (GLA_CHUNK, GLA_CHUNK), 1)
    if reverse:
        sum_mat = (c_idx >= r_idx).astype(BF16)
        keep = c_idx > r_idx
    else:
        sum_mat = (c_idx <= r_idx).astype(BF16)
        keep = c_idx <= r_idx
    n_chunks = GLA_STEP // GLA_CHUNK
    order = range(n_chunks - 1, -1, -1) if reverse else range(n_chunks)
    outs = [None] * n_chunks
    for c in order:
        rows = slice(c * GLA_CHUNK, (c + 1) * GLA_CHUNK)
        la_hi, la_mid, la_lo = _split3(la_ref[rows, :])
        b = _dot(sum_mat, la_hi) + _dot(sum_mat, la_mid) + _dot(sum_mat, la_lo)
        edge = GLA_CHUNK - 1
        total = b[0:1, :] if reverse else b[edge:edge + 1, :]
        q = q_ref[rows, :].astype(F32) * (GLA_DK ** -0.5)
        k = k_ref[rows, :].astype(F32)
        v = v_ref[rows, :]
        q_dec = (q * jnp.exp(b)).astype(BF16)
        k_inv = (k * jnp.exp(-b)).astype(BF16)
        k_end = (k * jnp.exp(total - b)).astype(BF16)
        a = jnp.where(keep, _dot_nt(q_dec, k_inv), 0.0).astype(BF16)
        st = st_ref[...]
        outs[c] = _dot(a, v) + _dot_nt(q_dec, st.astype(BF16))
        st_ref[...] = st * jnp.exp(total) + _dot_tn(v, k_end)
    return jnp.concatenate(outs, axis=0)


def _gla_fwd_kernel(q_ref, k_ref, v_ref, la_ref, o_ref, st_ref):
    @pl.when(pl.program_id(2) == 0)
    def _():
        st_ref[...] = jnp.zeros_like(st_ref)
    o_ref[...] = _gla_chunks(q_ref, k_ref, v_ref, la_ref, st_ref, reverse=False)


def _gla_bwd_kernel(q_ref, k_ref, v_ref, la_ref, of_ref, r_ref, ng_ref, o_ref, st_ref):
    @pl.when(pl.program_id(2) == 0)
    def _():
        st_ref[...] = jnp.zeros_like(st_ref)
    o = of_ref[...] + _gla_chunks(q_ref, k_ref, v_ref, la_ref, st_ref, reverse=True)
    ms = jnp.mean(o * o, axis=-1, keepdims=True)
    o = o * lax.rsqrt(ms + EPS) * ng_ref[...]
    o_ref[...] = (o * _silu(r_ref[...].astype(F32))).astype(BF16)


def _gla(proj, la_f, la_b, norm_g, batch, seq_len):
    t = proj.shape[0]
    nblk = seq_len // GLA_STEP
    k_col = GLA_HEADS
    v_col = 2 * GLA_HEADS * GLA_DK // GLA_DV
    r_col = v_col + GLA_HEADS

    def specs(row):
        return [
            pl.BlockSpec((GLA_STEP, GLA_DK), lambda b, h, n: (row(b, n), h)),
            pl.BlockSpec((GLA_STEP, GLA_DK), lambda b, h, n: (row(b, n), k_col + h)),
            pl.BlockSpec((GLA_STEP, GLA_DV), lambda b, h, n: (row(b, n), v_col + h)),
            pl.BlockSpec((GLA_STEP, GLA_DK), lambda b, h, n: (row(b, n), h)),
        ]

    def out_spec(row):
        return pl.BlockSpec((GLA_STEP, GLA_DV), lambda b, h, n: (row(b, n), h))

    def fwd_row(b, n):
        return b * nblk + n

    def bwd_row(b, n):
        return b * nblk + (nblk - 1 - n)

    grid = (batch, GLA_HEADS, nblk)
    state = pltpu.VMEM((GLA_DV, GLA_DK), F32)
    sem = ("arbitrary", "arbitrary", "arbitrary")
    o_f = pl.pallas_call(
        _gla_fwd_kernel, grid=grid, in_specs=specs(fwd_row), out_specs=out_spec(fwd_row),
        out_shape=jax.ShapeDtypeStruct((t, GLA_HEADS * GLA_DV), F32),
        scratch_shapes=[state], compiler_params=_params(sem), name="gla_forward",
    )(proj, proj, proj, la_f)
    return pl.pallas_call(
        _gla_bwd_kernel, grid=grid,
        in_specs=specs(bwd_row) + [
            out_spec(bwd_row),
            pl.BlockSpec((GLA_STEP, GLA_DV), lambda b, h, n: (bwd_row(b, n), r_col + h)),
            pl.BlockSpec((1, GLA_DV), lambda b, h, n: (0, 0)),
        ],
        out_specs=out_spec(bwd_row),
        out_shape=jax.ShapeDtypeStruct((t, GLA_HEADS * GLA_DV), BF16),
        scratch_shapes=[state], compiler_params=_params(sem), name="gla_backward",
    )(proj, proj, proj, la_b, o_f, proj, norm_g.reshape(1, GLA_DV))


ROUTE_TILE = 512


def _route_kernel(lg_ref, bias_ref, idx_ref, w_ref, rank_ref, cnt_ref, run_ref, tri_ref):
    tm = ROUTE_TILE

    @pl.when(pl.program_id(0) == 0)
    def _():
        run_ref[...] = jnp.zeros_like(run_ref)
        r = lax.broadcasted_iota(I32, (tm, tm), 0)
        c = lax.broadcasted_iota(I32, (tm, tm), 1)
        tri_ref[...] = (r < c).astype(BF16)

    shape3 = (N_GROUPS, GROUP_SIZE, tm)
    scores = _sigmoid(lg_ref[...])
    biased = (scores + bias_ref[...]).reshape(shape3)
    scores3 = scores.reshape(shape3)
    in_grp = lax.broadcasted_iota(I32, shape3, 1)
    grp_id = lax.broadcasted_iota(I32, shape3, 0)
    exp_id = grp_id * GROUP_SIZE + in_grp

    m1 = jnp.max(biased, axis=1, keepdims=True)
    first = jnp.min(jnp.where(biased == m1, in_grp, GROUP_SIZE), axis=1, keepdims=True)
    m2 = jnp.max(jnp.where(in_grp == first, NEG_INF, biased), axis=1, keepdims=True)
    grp_score = m1 + m2

    g_iota = lax.broadcasted_iota(I32, (N_GROUPS, 1, tm), 0)
    g_sel = jnp.zeros((N_GROUPS, 1, tm), dtype=jnp.bool_)
    cur = grp_score
    for _ in range(TOPK_GROUPS):
        m = jnp.max(cur, axis=0, keepdims=True)
        pick = jnp.min(jnp.where(cur == m, g_iota, N_GROUPS), axis=0, keepdims=True)
        hit = g_iota == pick
        g_sel = g_sel | hit
        cur = jnp.where(hit, NEG_INF, cur)

    cand = jnp.where(g_sel, biased, NEG_INF)
    picks, weights = [], []
    sel = jnp.zeros(shape3, dtype=jnp.bool_)
    for _ in range(TOP_K):
        m = jnp.max(jnp.max(cand, axis=1, keepdims=True), axis=0, keepdims=True)
        pick = jnp.where(cand == m, exp_id, N_EXPERTS)
        pick = jnp.min(jnp.min(pick, axis=1, keepdims=True), axis=0, keepdims=True)
        hit = exp_id == pick
        sel = sel | hit
        cand = jnp.where(hit, NEG_INF, cand)
        w = jnp.where(hit, scores3, 0.0)
        weights.append(jnp.sum(jnp.sum(w, axis=1, keepdims=True), axis=0, keepdims=True))
        picks.append(pick)

    sel_f = jnp.where(sel, 1.0, 0.0).reshape(N_EXPERTS, tm)
    prefix = (_dot(sel_f.astype(BF16), tri_ref[...]) + run_ref[:, 0:1]).reshape(shape3)
    ranks = []
    for pick in picks:
        r = jnp.where(exp_id == pick, prefix, 0.0)
        ranks.append(jnp.sum(jnp.sum(r, axis=1, keepdims=True), axis=0, keepdims=True))
    new_run = run_ref[...] + jnp.sum(sel_f, axis=1, keepdims=True)
    run_ref[...] = new_run
    cnt_ref[...] = new_run.astype(I32)

    total = weights[0]
    for w in weights[1:]:
        total = total + w
    row = lax.broadcasted_iota(I32, (TOP_K, tm), 0)

    def stack(vals):
        out = jnp.zeros((TOP_K, tm), vals[0].dtype)
        for k, v in enumerate(vals):
            out = jnp.where(row == k, v.reshape(1, tm), out)
        return out

    idx_ref[...] = stack(picks)
    w_ref[...] = stack([w / total * ROUTED_SCALE for w in weights])
    rank_ref[...] = stack(ranks).astype(I32)


def _route(logits_t, bias):
    t = logits_t.shape[1]
    tm = ROUTE_TILE
    tok = pl.BlockSpec((TOP_K, tm), lambda i: (0, i))
    idx, w, rank, cnt = pl.pallas_call(
        _route_kernel,
        grid=(t // tm,),
        in_specs=[pl.BlockSpec((N_EXPERTS, tm), lambda i: (0, i)),
                  pl.BlockSpec((N_EXPERTS, 1), lambda i: (0, 0))],
        out_specs=[tok, tok, tok, pl.BlockSpec((N_EXPERTS, LANES), lambda i: (0, 0))],
        out_shape=[jax.ShapeDtypeStruct((TOP_K, t), I32),
                   jax.ShapeDtypeStruct((TOP_K, t), F32),
                   jax.ShapeDtypeStruct((TOP_K, t), I32),
                   jax.ShapeDtypeStruct((N_EXPERTS, LANES), I32)],
        scratch_shapes=[pltpu.VMEM((N_EXPERTS, LANES), F32), pltpu.VMEM((tm, tm), BF16)],
        compiler_params=_params(("arbitrary",)),
        name="moe_route",
    )(logits_t, bias.reshape(N_EXPERTS, 1))
    return idx, w, rank, cnt[:, 0]


DISPATCH_TILE = 512


def _token_rows(i):
    return pl.ds(pl.multiple_of(i * TOKEN_TILE_ROWS, TOKEN_TILE_ROWS), TOKEN_TILE_ROWS)


def _block_rows(b):
    n = SLOT_BLOCK * TOKEN_TILE_ROWS
    return pl.ds(pl.multiple_of(b * n, n), n)


def _dispatch_kernel(cnt_ref, start_ref, nu_ref, pos_ref, h3_ref, x3_ref, zero_ref, sem, pad_sem,
                     *, n_blocks):
    i = pl.program_id(0)
    tm = DISPATCH_TILE

    def issue(t, carry):
        for k in range(TOP_K):
            pltpu.make_async_copy(h3_ref.at[_token_rows(t)],
                                  x3_ref.at[_token_rows(pos_ref[k, t])], sem).start()
        return carry

    lax.fori_loop(0, tm, issue, 0)

    @pl.when(i == 0)
    def _():
        zero_ref[...] = jnp.zeros_like(zero_ref)
        zero_tile = zero_ref.at[pl.ds(0, TOKEN_TILE_ROWS)]

        def per_expert(e, n_pad):
            cnt = cnt_ref[e]
            padded = (cnt + SLOT_BLOCK - 1) // SLOT_BLOCK * SLOT_BLOCK

            def pad_row(r, c):
                pltpu.make_async_copy(zero_tile, x3_ref.at[_token_rows(start_ref[e] + r)],
                                      pad_sem).start()
                return c

            lax.fori_loop(cnt, padded, pad_row, 0)
            return n_pad + (padded - cnt)

        n_pad = lax.fori_loop(0, N_EXPERTS, per_expert, 0)

        def tail_block(b, c):
            pltpu.make_async_copy(zero_ref, x3_ref.at[_block_rows(b)], pad_sem).start()
            return c

        lax.fori_loop(nu_ref[0], n_blocks, tail_block, 0)

        def pad_wait(r, c):
            pltpu.make_async_copy(zero_tile, x3_ref.at[_token_rows(0)], pad_sem).wait()
            return c

        lax.fori_loop(0, n_pad, pad_wait, 0)

        def tail_wait(b, c):
            pltpu.make_async_copy(zero_ref, x3_ref.at[_block_rows(0)], pad_sem).wait()
            return c

        lax.fori_loop(nu_ref[0], n_blocks, tail_wait, 0)

    n_rows = tm * TOP_K * TOKEN_TILE_ROWS
    pltpu.make_async_copy(x3_ref.at[pl.ds(0, n_rows)], x3_ref.at[pl.ds(0, n_rows)], sem).wait()


def _dispatch(h3, pos, counts, starts, n_used, n_blocks):
    t = pos.shape[1]
    tm = DISPATCH_TILE
    return pl.pallas_call(
        functools.partial(_dispatch_kernel, n_blocks=n_blocks),
        grid_spec=pltpu.PrefetchScalarGridSpec(
            num_scalar_prefetch=3,
            grid=(t // tm,),
            in_specs=[pl.BlockSpec((TOP_K, tm), lambda i, c, s, u: (0, i), memory_space=pltpu.SMEM),
                      pl.BlockSpec((tm * TOKEN_TILE_ROWS, LANES), lambda i, c, s, u: (i, 0))],
            out_specs=pl.BlockSpec(memory_space=pl.ANY),
            scratch_shapes=[pltpu.VMEM((SLOT_BLOCK * TOKEN_TILE_ROWS, LANES), I32),
                            pltpu.SemaphoreType.DMA, pltpu.SemaphoreType.DMA],
        ),
        out_shape=jax.ShapeDtypeStruct((n_blocks * SLOT_BLOCK * TOKEN_TILE_ROWS, LANES), I32),
        compiler_params=_params(("arbitrary",)),
        name="moe_dispatch",
    )(counts, starts, n_used, pos, h3)


def _expert_kernel(be_ref, nu_ref, x3_ref, wg_ref, wu_ref, wd_ref, y3_ref):
    @pl.when(pl.program_id(0) >= nu_ref[0])
    def _():
        y3_ref[...] = jnp.zeros_like(y3_ref)

    @pl.when(pl.program_id(0) < nu_ref[0])
    def _():
        m = SLOT_BLOCK
        pieces = []
        for c in range(TOKEN_TILE_ROWS):
            lo, hi = _unpack_pair(x3_ref[pl.ds(c, m, stride=TOKEN_TILE_ROWS), :])
            pieces += [lo.astype(BF16), hi.astype(BF16)]
        x = jnp.concatenate(pieces, axis=1)
        a = (_silu(_dot(x, wg_ref[0])) * _dot(x, wu_ref[0])).astype(BF16)
        y = _dot(a, wd_ref[0])
        for c in range(TOKEN_TILE_ROWS):
            base = 2 * LANES * c
            y3_ref[pl.ds(c, m, stride=TOKEN_TILE_ROWS), :] = _pack_pair(
                y[:, base:base + LANES], y[:, base + LANES:base + 2 * LANES])


def _experts(x3, block_expert, n_used, w_gate, w_up, w_down):
    n_blocks = block_expert.shape[0]
    rows = SLOT_BLOCK * TOKEN_TILE_ROWS

    def blk(b, be, nu):
        return jnp.minimum(b, nu[0] - 1)

    return pl.pallas_call(
        _expert_kernel,
        grid_spec=pltpu.PrefetchScalarGridSpec(
            num_scalar_prefetch=2,
            grid=(n_blocks,),
            in_specs=[
                pl.BlockSpec((rows, LANES), lambda b, be, nu: (blk(b, be, nu), 0)),
                pl.BlockSpec((1, D_MODEL, EXPERT_DIM), lambda b, be, nu: (be[blk(b, be, nu)], 0, 0)),
                pl.BlockSpec((1, D_MODEL, EXPERT_DIM), lambda b, be, nu: (be[blk(b, be, nu)], 0, 0)),
                pl.BlockSpec((1, EXPERT_DIM, D_MODEL), lambda b, be, nu: (be[blk(b, be, nu)], 0, 0)),
            ],
            out_specs=pl.BlockSpec((rows, LANES), lambda b, be, nu: (b, 0)),
        ),
        out_shape=jax.ShapeDtypeStruct(x3.shape, I32),
        compiler_params=_params(("arbitrary",)),
        name="moe_experts",
    )(block_expert, n_used, x3, w_gate, w_up, w_down)


COMBINE_TILE = 256


def _combine_kernel(pos_ref, x_ref, h_ref, w_ref, gate_ref, sg_ref, su_ref, sd_ref, y3_ref,
                    o_ref, buf_ref, sem):
    tm = COMBINE_TILE

    def issue(t, carry):
        for k in range(TOP_K):
            pltpu.make_async_copy(y3_ref.at[_token_rows(pos_ref[k, t])],
                                  buf_ref.at[_token_rows(k * tm + t)], sem).start()
        return carry

    lax.fori_loop(0, tm, issue, 0)

    h = h_ref[...]
    a = (_silu(_dot(h, sg_ref[...])) * _dot(h, su_ref[...])).astype(BF16)
    shared = _dot(a, sd_ref[...])

    pltpu.make_async_copy(y3_ref.at[pl.ds(0, buf_ref.shape[0])], buf_ref, sem).wait()

    gate = gate_ref[0]
    wk = [w_ref[:, k:k + 1] for k in range(TOP_K)]
    for c in range(TOKEN_TILE_ROWS):
        lo_acc = None
        for k in range(TOP_K):
            word = buf_ref[pl.ds(k * tm * TOKEN_TILE_ROWS + c, tm, stride=TOKEN_TILE_ROWS), :]
            lo, hi = _unpack_pair(word)
            if lo_acc is None:
                lo_acc, hi_acc = wk[k] * lo, wk[k] * hi
            else:
                lo_acc, hi_acc = lo_acc + wk[k] * lo, hi_acc + wk[k] * hi
        base = 2 * LANES * c
        for off, acc in ((base, lo_acc), (base + LANES, hi_acc)):
            cols = slice(off, off + LANES)
            o_ref[:, cols] = x_ref[:, cols] + gate[:, cols] * (acc + shared[:, cols])


def _combine(x, h, y3, pos, w_tok, mod3, gate_slot, seq_len, s_gate, s_up, s_down):
    t = x.shape[0]
    tm = COMBINE_TILE
    sps = seq_len // tm
    full = lambda shape: pl.BlockSpec(shape, lambda i: tuple(0 for _ in shape))
    return pl.pallas_call(
        _combine_kernel,
        grid=(t // tm,),
        in_specs=[
            pl.BlockSpec((TOP_K, tm), lambda i: (0, i), memory_space=pltpu.SMEM),
            pl.BlockSpec((tm, D_MODEL), lambda i: (i, 0)),
            pl.BlockSpec((tm, D_MODEL), lambda i: (i, 0)),
            pl.BlockSpec((tm, TOP_K), lambda i: (i, 0)),
            _mod_spec(gate_slot, sps),
            full((D_MODEL, EXPERT_DIM)), full((D_MODEL, EXPERT_DIM)), full((EXPERT_DIM, D_MODEL)),
            pl.BlockSpec(memory_space=pl.ANY),
        ],
        out_specs=pl.BlockSpec((tm, D_MODEL), lambda i: (i, 0)),
        out_shape=jax.ShapeDtypeStruct((t, D_MODEL), F32),
        scratch_shapes=[pltpu.VMEM((TOP_K * tm * TOKEN_TILE_ROWS, LANES), I32),
                        pltpu.SemaphoreType.DMA],
        compiler_params=_params(("arbitrary",)),
        name="moe_combine",
    )(pos, x, h, w_tok, mod3, s_gate, s_up, s_down, y3)


def _moe_layer(x, mod3, norm_g, seq_len, router_wt, router_bias, w_gate, w_up, w_down,
               s_gate, s_up, s_down):
    t = x.shape[0]
    h, h3, logits_t = _norm_mod(x, norm_g, mod3, 3, 4, seq_len, router_wt)
    idx, w, rank, counts = _route(logits_t, router_bias)
    n_blocks = -(-(t * TOP_K + N_EXPERTS * (SLOT_BLOCK - 1)) // SLOT_BLOCK)
    padded = (counts + SLOT_BLOCK - 1) // SLOT_BLOCK * SLOT_BLOCK
    padded_end = jnp.cumsum(padded)
    starts = (padded_end - padded).astype(I32)
    block_start = jnp.arange(n_blocks, dtype=I32) * SLOT_BLOCK
    block_expert = jnp.minimum(
        jnp.sum((padded_end[None, :] <= block_start[:, None]).astype(I32), axis=1), N_EXPERTS - 1)
    n_used = (padded_end[-1:] // SLOT_BLOCK).astype(I32)
    experts = jnp.arange(N_EXPERTS, dtype=I32)
    pos = jnp.sum(jnp.where(idx[:, :, None] == experts, starts, 0), axis=-1) + rank
    x3 = _dispatch(h3, pos, counts.astype(I32), starts, n_used, n_blocks)
    y3 = _experts(x3, block_expert, n_used, w_gate, w_up, w_down)
    return _combine(x, h, y3, pos, w.T, mod3, 5, seq_len, s_gate, s_up, s_down)


def _trunk(x, mod, batch, seq_len, p):
    t = batch * seq_len
    x = x.reshape(t, D_MODEL)
    for layer in range(DEPTH):
        mod3 = mod[layer].reshape(batch * 6, 1, D_MODEL)
        g_mix = p["norm_mix"][layer].reshape(1, D_MODEL)
        g_ffn = p["norm_ffn"][layer].reshape(1, D_MODEL)
        h = _norm_mod(x, g_mix, mod3, 0, 1, seq_len)
        if layer % 2 == 0:
            a = layer // 2
            qkv = _matmul(h, p["attn_w_in"][a])
            o = _attention(qkv, p["attn_q_norm"][a], p["attn_k_norm"][a], p["attn_sink"][a],
                           batch, seq_len)
            x = _matmul_residual(o, p["attn_w_out"][a], x, mod3, 2, seq_len)
        else:
            g = layer // 2
            proj = _matmul(h, p["gla_w_main"][g])
            la_f, la_b = _gla_gates(h, p["gla_w_lr"][g], p["gla_w_g"][g], p["gla_b_g"][g])
            o = _gla(proj, la_f, la_b, p["gla_norm"][g], batch, seq_len)
            x = _matmul_residual(o, p["gla_w_out"][g], x, mod3, 2, seq_len)
        x = _moe_layer(x, mod3, g_ffn, seq_len, p["router_wt"][layer], p["router_bias"][layer],
                       p["exp_w_gate"][layer], p["exp_w_up"][layer], p["exp_w_down"][layer],
                       p["sh_w_gate"][layer], p["sh_w_up"][layer], p["sh_w_down"][layer])
    return x.reshape(batch, seq_len, D_MODEL)


def kernel(x_prompt, x_sample, c_prompt, c_sample, ada_w, ada_b, norm_mix, norm_ffn, attn_w_in, attn_q_norm, attn_k_norm, attn_sink, attn_w_out, gla_w_in, gla_w_g2_f, gla_b_g_f, gla_w_g2_b, gla_b_g_b, gla_norm, gla_w_out, router_w, router_bias, exp_w_gate, exp_w_up, exp_w_down, sh_w_gate, sh_w_up, sh_w_down):
    bp, sp, _ = x_prompt.shape
    bs, ss, _ = x_sample.shape
    n_gate = GLA_HEADS * GLA_DK
    n_layers_gla = gla_w_in.shape[0]

    w_lr = jnp.zeros((n_layers_gla, D_MODEL, GLA_GATE_PAD), F32)
    w_lr = w_lr.at[:, :, :2 * GLA_RANK].set(gla_w_in[:, :, GLA_MAIN:])
    w_g = jnp.zeros((n_layers_gla, GLA_GATE_PAD, 2 * n_gate), F32)
    w_g = w_g.at[:, :GLA_RANK, :n_gate].set(gla_w_g2_f)
    w_g = w_g.at[:, GLA_RANK:2 * GLA_RANK, n_gate:].set(gla_w_g2_b)
    p = {
        "norm_mix": norm_mix, "norm_ffn": norm_ffn,
        "attn_w_in": attn_w_in.astype(BF16), "attn_q_norm": attn_q_norm,
        "attn_k_norm": attn_k_norm, "attn_sink": attn_sink, "attn_w_out": attn_w_out.astype(BF16),
        "gla_w_main": gla_w_in[:, :, :GLA_MAIN].astype(BF16), "gla_w_lr": w_lr, "gla_w_g": w_g,
        "gla_b_g": jnp.concatenate([gla_b_g_f, gla_b_g_b], axis=-1).reshape(n_layers_gla, 1, 2 * n_gate),
        "gla_norm": gla_norm, "gla_w_out": gla_w_out.astype(BF16),
        "router_wt": jnp.swapaxes(router_w, 1, 2), "router_bias": router_bias,
        "exp_w_gate": exp_w_gate.astype(BF16), "exp_w_up": exp_w_up.astype(BF16),
        "exp_w_down": exp_w_down.astype(BF16),
        "sh_w_gate": sh_w_gate.astype(BF16), "sh_w_up": sh_w_up.astype(BF16),
        "sh_w_down": sh_w_down.astype(BF16),
    }

    c_all = jnp.zeros((SUBLANES, D_MODEL), F32).at[:bp + bs].set(jnp.concatenate([c_prompt, c_sample], 0))
    mod = _modulation(c_all, ada_w, ada_b)
    y_prompt = _trunk(x_prompt, mod[:, :bp], bp, sp, p)
    y_sample = _trunk(x_sample, mod[:, bp:bp + bs], bs, ss, p)
    return (y_prompt, y_sample)
```

```python
import functools

import numpy as np
import jax
import jax.numpy as jnp
from jax import lax
from jax.experimental import pallas as pl
from jax.experimental.pallas import tpu as pltpu

F32 = jnp.float32
BF16 = jnp.bfloat16
I32 = jnp.int32

D_MODEL = 2048
DEPTH = 2
EPS = 1e-6
HEAD_DIM = 128
N_Q_HEADS = 16
N_KV_HEADS = 4
Q_PER_KV = 4
WINDOW = 128
ATTN_BLOCK = 128
GLA_HEADS = 4
GLA_DK = 256
GLA_DV = 512
GLA_RANK = 16
GLA_TAU = 16.0
GLA_CHUNK = 64
GLA_MAIN = 2 * GLA_HEADS * GLA_DK + 2 * GLA_HEADS * GLA_DV
N_EXPERTS = 64
TOP_K = 8
N_GROUPS = 8
GROUP_SIZE = N_EXPERTS // N_GROUPS
TOPK_GROUPS = 4
EXPERT_DIM = 512
ROUTED_SCALE = 2.5
SLOT_BLOCK = 256

SUBLANES = 8
LANES = 128
TOKEN_TILE_ROWS = D_MODEL // (2 * LANES)
VMEM_LIMIT_BYTES = 56 * 1024 * 1024
N_DMA_PRIORITIES = 2

NEG_INF = float("-inf")


def _params(semantics):
    return pltpu.CompilerParams(dimension_semantics=semantics, vmem_limit_bytes=VMEM_LIMIT_BYTES)


def _dot(a, b):
    return jnp.dot(a, b, preferred_element_type=F32)


def _dot_nt(a, b):
    return lax.dot_general(a, b, (((1,), (1,)), ((), ())), preferred_element_type=F32)


def _dot_tn(a, b):
    return lax.dot_general(a, b, (((0,), (0,)), ((), ())), preferred_element_type=F32)


def _split3(x):
    hi = x.astype(BF16)
    r1 = x - hi.astype(F32)
    mid = r1.astype(BF16)
    lo = (r1 - mid.astype(F32)).astype(BF16)
    return hi, mid, lo


def _sigmoid(x):
    return 1.0 / (1.0 + jnp.exp(-x))


def _silu(x):
    return x * _sigmoid(x)


def _pack_pair(lo, hi):
    lo_bits = lax.bitcast_convert_type(lo.astype(BF16).astype(F32), I32)
    hi_bits = lax.bitcast_convert_type(hi.astype(BF16).astype(F32), I32)
    return lax.shift_right_logical(lo_bits, 16) | (hi_bits & jnp.int32(-65536))


def _unpack_pair(word):
    lo = lax.bitcast_convert_type(lax.shift_left(word, 16), F32)
    hi = lax.bitcast_convert_type(word & jnp.int32(-65536), F32)
    return lo, hi


def _mod_kernel(c_ref, w_ref, b_ref, o_ref):
    a = _silu(c_ref[...])
    o_ref[0] = jnp.dot(a, w_ref[0], preferred_element_type=F32,
                       precision=lax.Precision.HIGHEST) + b_ref[0]


def _modulation(c_all, ada_w, ada_b):
    n = 6 * D_MODEL
    tn = 1024
    return pl.pallas_call(
        _mod_kernel,
        grid=(DEPTH, n // tn),
        in_specs=[
            pl.BlockSpec((SUBLANES, D_MODEL), lambda l, j: (0, 0)),
            pl.BlockSpec((1, D_MODEL, tn), lambda l, j: (l, 0, j)),
            pl.BlockSpec((1, 1, tn), lambda l, j: (l, 0, j)),
        ],
        out_specs=pl.BlockSpec((1, SUBLANES, tn), lambda l, j: (l, 0, j)),
        out_shape=jax.ShapeDtypeStruct((DEPTH, SUBLANES, n), F32),
        compiler_params=_params(("arbitrary", "arbitrary")),
        name="adaln_mod",
    )(c_all, ada_w, ada_b.reshape(DEPTH, 1, n))


def _normed(x_ref, g_ref, sc_ref, sh_ref):
    x = x_ref[...]
    ms = jnp.mean(x * x, axis=-1, keepdims=True)
    y = x * lax.rsqrt(ms + EPS) * g_ref[...]
    return y * (1.0 + sc_ref[0]) + sh_ref[0]


def _norm_kernel(x_ref, g_ref, sc_ref, sh_ref, h_ref):
    h_ref[...] = _normed(x_ref, g_ref, sc_ref, sh_ref).astype(BF16)


def _norm_router_kernel(x_ref, g_ref, sc_ref, sh_ref, wr_ref, h_ref, h3_ref, lg_ref):
    h = _normed(x_ref, g_ref, sc_ref, sh_ref)
    tm = h.shape[0]
    h_hi = h.astype(BF16)
    h_ref[...] = h_hi
    h_lo = (h - h_hi.astype(F32)).astype(BF16)
    w_hi, w_mid, _ = _split3(wr_ref[...])
    lg_ref[...] = _dot_nt(w_hi, h_hi) + _dot_nt(w_hi, h_lo) + _dot_nt(w_mid, h_hi)
    h_r = h_hi.astype(F32)
    for c in range(TOKEN_TILE_ROWS):
        base = 2 * LANES * c
        word = _pack_pair(h_r[:, base:base + LANES], h_r[:, base + LANES:base + 2 * LANES])
        h3_ref[pl.ds(c, tm, stride=TOKEN_TILE_ROWS), :] = word


def _mod_spec(slot, steps_per_seq, width=D_MODEL):
    return pl.BlockSpec((1, 1, width), lambda i: (6 * (i // steps_per_seq) + slot, 0, 0))


def _norm_mod(x, g, mod3, shift_slot, scale_slot, seq_len, router_wt=None):
    t = x.shape[0]
    tm = 512
    sps = seq_len // tm
    in_specs = [
        pl.BlockSpec((tm, D_MODEL), lambda i: (i, 0)),
        pl.BlockSpec((1, D_MODEL), lambda i: (0, 0)),
        _mod_spec(scale_slot, sps),
        _mod_spec(shift_slot, sps),
    ]
    h_spec = pl.BlockSpec((tm, D_MODEL), lambda i: (i, 0))
    h_shape = jax.ShapeDtypeStruct((t, D_MODEL), BF16)
    if router_wt is None:
        return pl.pallas_call(
            _norm_kernel, grid=(t // tm,), in_specs=in_specs, out_specs=h_spec, out_shape=h_shape,
            compiler_params=_params(("arbitrary",)), name="norm_mod",
        )(x, g, mod3, mod3)
    return pl.pallas_call(
        _norm_router_kernel,
        grid=(t // tm,),
        in_specs=in_specs + [pl.BlockSpec((N_EXPERTS, D_MODEL), lambda i: (0, 0))],
        out_specs=[
            h_spec,
            pl.BlockSpec((tm * TOKEN_TILE_ROWS, LANES), lambda i: (i, 0)),
            pl.BlockSpec((N_EXPERTS, tm), lambda i: (0, i)),
        ],
        out_shape=[
            h_shape,
            jax.ShapeDtypeStruct((t * TOKEN_TILE_ROWS, LANES), I32),
            jax.ShapeDtypeStruct((N_EXPERTS, t), F32),
        ],
        compiler_params=_params(("arbitrary",)),
        name="norm_mod_router",
    )(x, g, mod3, mod3, router_wt)


def _mm_kernel(a_ref, w_ref, o_ref):
    o_ref[...] = _dot(a_ref[...], w_ref[...]).astype(o_ref.dtype)


def _mm_res_kernel(a_ref, w_ref, x_ref, gate_ref, o_ref):
    o_ref[...] = x_ref[...] + gate_ref[0] * _dot(a_ref[...], w_ref[...])


def _matmul(a, w, out_dtype=BF16, tm=1024, tn=512):
    m, k = a.shape
    n = w.shape[1]
    return pl.pallas_call(
        _mm_kernel,
        grid=(m // tm, n // tn),
        in_specs=[pl.BlockSpec((tm, k), lambda i, j: (i, 0)),
                  pl.BlockSpec((k, tn), lambda i, j: (0, j))],
        out_specs=pl.BlockSpec((tm, tn), lambda i, j: (i, j)),
        out_shape=jax.ShapeDtypeStruct((m, n), out_dtype),
        compiler_params=_params(("arbitrary", "arbitrary")),
        name="matmul",
    )(a, w)


def _matmul_residual(a, w, x, mod3, gate_slot, seq_len, tm=1024, tn=512):
    m, k = a.shape
    n = w.shape[1]
    sps = seq_len // tm
    return pl.pallas_call(
        _mm_res_kernel,
        grid=(m // tm, n // tn),
        in_specs=[pl.BlockSpec((tm, k), lambda i, j: (i, 0)),
                  pl.BlockSpec((k, tn), lambda i, j: (0, j)),
                  pl.BlockSpec((tm, tn), lambda i, j: (i, j)),
                  pl.BlockSpec((1, 1, tn), lambda i, j: (6 * (i // sps) + gate_slot, 0, j))],
        out_specs=pl.BlockSpec((tm, tn), lambda i, j: (i, j)),
        out_shape=jax.ShapeDtypeStruct((m, n), F32),
        compiler_params=_params(("arbitrary", "arbitrary")),
        name="matmul_residual",
    )(a, w, x, mod3)


_ALIBI_SLOPES = [float(s) for s in
                 np.exp2(-8.0 * np.arange(1, N_Q_HEADS + 1, dtype=np.float32) / N_Q_HEADS)]


def _head_norm(x, gain):
    xf = x.astype(F32)
    ms = jnp.mean(xf * xf, axis=-1, keepdims=True)
    return xf * lax.rsqrt(ms + EPS) * gain


def _attn_kernel(sink_ref, q_ref, kp_ref, kc_ref, kn_ref, vp_ref, vc_ref, vn_ref,
                 qg_ref, kg_ref, o_ref, *, n_blocks):
    i = pl.program_id(1)
    span = ATTN_BLOCK + 2 * WINDOW
    t_idx = lax.broadcasted_iota(I32, (ATTN_BLOCK, span), 0)
    j_idx = lax.broadcasted_iota(I32, (ATTN_BLOCK, span), 1)
    dist = jnp.abs(t_idx - j_idx + WINDOW)
    valid = dist <= WINDOW
    valid = valid & ((j_idx >= WINDOW) | (i > 0))
    valid = valid & ((j_idx < WINDOW + ATTN_BLOCK) | (i < n_blocks - 1))
    dist_f = dist.astype(F32)
    q_gain = qg_ref[...] * (HEAD_DIM ** -0.5)
    k_gain = kg_ref[...]
    for g in range(N_KV_HEADS):
        cols = slice(g * HEAD_DIM, (g + 1) * HEAD_DIM)
        k_all = jnp.concatenate([kp_ref[:, cols], kc_ref[:, cols], kn_ref[:, cols]], axis=0)
        v_all = jnp.concatenate([vp_ref[:, cols], vc_ref[:, cols], vn_ref[:, cols]], axis=0)
        k_n = _head_norm(k_all, k_gain).astype(BF16)
        q_n = []
        for jh in range(Q_PER_KV):
            h = g * Q_PER_KV + jh
            q_n.append(_head_norm(q_ref[:, h * HEAD_DIM:(h + 1) * HEAD_DIM], q_gain).astype(BF16))
        s_all = _dot_nt(jnp.concatenate(q_n, axis=0), k_n)
        p_all, inv_den = [], []
        for jh in range(Q_PER_KV):
            h = g * Q_PER_KV + jh
            sink = sink_ref[h]
            s = s_all[jh * ATTN_BLOCK:(jh + 1) * ATTN_BLOCK] - _ALIBI_SLOPES[h] * dist_f
            s = jnp.where(valid, s, NEG_INF)
            m = jnp.maximum(jnp.max(s, axis=-1, keepdims=True), sink)
            p = jnp.exp(s - m)
            den = jnp.sum(p, axis=-1, keepdims=True) + jnp.exp(sink - m)
            p_all.append(p.astype(BF16))
            inv_den.append(1.0 / den)
        o_all = _dot(jnp.concatenate(p_all, axis=0), v_all)
        outs = [o_all[jh * ATTN_BLOCK:(jh + 1) * ATTN_BLOCK] * inv_den[jh] for jh in range(Q_PER_KV)]
        o_ref[:, g * Q_PER_KV * HEAD_DIM:(g + 1) * Q_PER_KV * HEAD_DIM] = (
            jnp.concatenate(outs, axis=1).astype(BF16))


def _attention(qkv, q_gain, k_gain, sink, batch, seq_len):
    t = qkv.shape[0]
    nb = seq_len // ATTN_BLOCK
    qw = N_Q_HEADS * HEAD_DIM
    kvw = N_KV_HEADS * HEAD_DIM
    k_col = qw // kvw
    v_col = k_col + 1

    def rows(b, i):
        return b * nb + i

    def kv_spec(col, shift):
        return pl.BlockSpec(
            (ATTN_BLOCK, kvw),
            lambda b, i: (rows(b, jnp.clip(i + shift, 0, nb - 1)), col))

    return pl.pallas_call(
        functools.partial(_attn_kernel, n_blocks=nb),
        grid=(batch, nb),
        in_specs=[
            pl.BlockSpec(memory_space=pltpu.SMEM),
            pl.BlockSpec((ATTN_BLOCK, qw), lambda b, i: (rows(b, i), 0)),
            kv_spec(k_col, -1), kv_spec(k_col, 0), kv_spec(k_col, 1),
            kv_spec(v_col, -1), kv_spec(v_col, 0), kv_spec(v_col, 1),
            pl.BlockSpec((1, HEAD_DIM), lambda b, i: (0, 0)),
            pl.BlockSpec((1, HEAD_DIM), lambda b, i: (0, 0)),
        ],
        out_specs=pl.BlockSpec((ATTN_BLOCK, qw), lambda b, i: (rows(b, i), 0)),
        out_shape=jax.ShapeDtypeStruct((t, qw), BF16),
        compiler_params=_params(("arbitrary", "arbitrary")),
        name="window_attention",
    )(sink, qkv, qkv, qkv, qkv, qkv, qkv, qkv, q_gain.reshape(1, HEAD_DIM), k_gain.reshape(1, HEAD_DIM))


GLA_GATE_PAD = LANES


def _gla_gate_kernel(h_ref, wlr_ref, wg_ref, bg_ref, laf_ref, lab_ref):
    h = h_ref[...]
    w_hi, w_mid, _ = _split3(wlr_ref[...])
    lr = _dot(h, w_hi) + _dot(h, w_mid)
    l_hi, l_mid, l_lo = _split3(lr)
    g_hi, g_mid, _ = _split3(wg_ref[...])
    z = (_dot(l_hi, g_hi) + _dot(l_mid, g_hi) + _dot(l_hi, g_mid) + _dot(l_lo, g_hi)
         + bg_ref[...])
    log_sig = jnp.minimum(z, 0.0) - jnp.log(1.0 + jnp.exp(-jnp.abs(z)))
    la = log_sig / GLA_TAU
    n = GLA_HEADS * GLA_DK
    laf_ref[...] = la[:, :n]
    lab_ref[...] = la[:, n:]


def _gla_gates(h, w_lr, w_g, b_g):
    t = h.shape[0]
    tm = 512
    n = GLA_HEADS * GLA_DK
    out = jax.ShapeDtypeStruct((t, n), F32)
    return pl.pallas_call(
        _gla_gate_kernel,
        grid=(t // tm,),
        in_specs=[pl.BlockSpec((tm, D_MODEL), lambda i: (i, 0)),
                  pl.BlockSpec((D_MODEL, GLA_GATE_PAD), lambda i: (0, 0)),
                  pl.BlockSpec((GLA_GATE_PAD, 2 * n), lambda i: (0, 0)),
                  pl.BlockSpec((1, 2 * n), lambda i: (0, 0))],
        out_specs=[pl.BlockSpec((tm, n), lambda i: (i, 0)), pl.BlockSpec((tm, n), lambda i: (i, 0))],
        out_shape=[out, out],
        compiler_params=_params(("arbitrary",)),
        name="gla_gates",
    )(h, w_lr, w_g, b_g)


GLA_STEP = 256


def _gla_chunks(q_ref, k_ref, v_ref, la_ref, st_ref, reverse):
    shift = GLA_CHUNK.bit_length() - 1
    r_idx = lax.broadcasted_iota(I32, (GLA_STEP, GLA_STEP), 0)
    c_idx = lax.broadcasted_iota(I32, (GLA_STEP, GLA_STEP), 1)
    same_chunk = lax.shift_right_logical(r_idx, shift) == lax.shift_right_logical(c_idx, shift)
    if reverse:
        sum_mat = same_chunk & (c_idx >= r_idx)
        keep = same_chunk & (c_idx > r_idx)
    else:
        sum_mat = same_chunk & (c_idx <= r_idx)
        keep = sum_mat
    sum_mat = jnp.where(sum_mat, 1.0, 0.0).astype(BF16)
    chunk_mat = jnp.where(same_chunk, 1.0, 0.0).astype(BF16)
    la_hi, la_mid, la_lo = _split3(la_ref[...])
    b = _dot(sum_mat, la_hi) + _dot(sum_mat, la_mid) + _dot(sum_mat, la_lo)
    total = _dot(chunk_mat, la_hi) + _dot(chunk_mat, la_mid) + _dot(chunk_mat, la_lo)
    q = q_ref[...].astype(F32) * (GLA_DK ** -0.5)
    k = k_ref[...].astype(F32)
    v = v_ref[...]
    q_dec = (q * jnp.exp(b)).astype(BF16)
    k_inv = (k * jnp.exp(-b)).astype(BF16)
    k_end = (k * jnp.exp(total - b)).astype(BF16)
    decay = jnp.exp(total)
    a = jnp.where(keep, _dot_nt(q_dec, k_inv), 0.0).astype(BF16)
    o_intra = _dot(a, v)
    n_chunks = GLA_STEP // GLA_CHUNK
    order = range(n_chunks - 1, -1, -1) if reverse else range(n_chunks)
    outs = [None] * n_chunks
    st = st_ref[...]
    for c in order:
        lo = c * GLA_CHUNK
        rows = slice(lo, lo + GLA_CHUNK)
        outs[c] = _dot_nt(q_dec[rows], st.astype(BF16))
        st = st * decay[lo:lo + 1, :] + _dot_tn(v[rows], k_end[rows])
    st_ref[...] = st
    return o_intra + jnp.concatenate(outs, axis=0)


def _gla_fwd_kernel(q_ref, k_ref, v_ref, la_ref, o_ref, st_ref):
    @pl.when(pl.program_id(2) == 0)
    def _():
        st_ref[...] = jnp.zeros_like(st_ref)
    o_ref[...] = _gla_chunks(q_ref, k_ref, v_ref, la_ref, st_ref, reverse=False)


def _gla_bwd_kernel(q_ref, k_ref, v_ref, la_ref, of_ref, r_ref, ng_ref, o_ref, st_ref):
    @pl.when(pl.program_id(2) == 0)
    def _():
        st_ref[...] = jnp.zeros_like(st_ref)
    o = of_ref[...] + _gla_chunks(q_ref, k_ref, v_ref, la_ref, st_ref, reverse=True)
    ms = jnp.mean(o * o, axis=-1, keepdims=True)
    o = o * lax.rsqrt(ms + EPS) * ng_ref[...]
    o_ref[...] = (o * _silu(r_ref[...].astype(F32))).astype(BF16)


def _gla(proj, la_f, la_b, norm_g, batch, seq_len):
    t = proj.shape[0]
    nblk = seq_len // GLA_STEP
    k_col = GLA_HEADS
    v_col = 2 * GLA_HEADS * GLA_DK // GLA_DV
    r_col = v_col + GLA_HEADS

    def specs(row):
        return [
            pl.BlockSpec((GLA_STEP, GLA_DK), lambda b, h, n: (row(b, n), h)),
            pl.BlockSpec((GLA_STEP, GLA_DK), lambda b, h, n: (row(b, n), k_col + h)),
            pl.BlockSpec((GLA_STEP, GLA_DV), lambda b, h, n: (row(b, n), v_col + h)),
            pl.BlockSpec((GLA_STEP, GLA_DK), lambda b, h, n: (row(b, n), h)),
        ]

    def out_spec(row):
        return pl.BlockSpec((GLA_STEP, GLA_DV), lambda b, h, n: (row(b, n), h))

    def fwd_row(b, n):
        return b * nblk + n

    def bwd_row(b, n):
        return b * nblk + (nblk - 1 - n)

    grid = (batch, GLA_HEADS, nblk)
    state = pltpu.VMEM((GLA_DV, GLA_DK), F32)
    sem = ("arbitrary", "arbitrary", "arbitrary")
    o_f = pl.pallas_call(
        _gla_fwd_kernel, grid=grid, in_specs=specs(fwd_row), out_specs=out_spec(fwd_row),
        out_shape=jax.ShapeDtypeStruct((t, GLA_HEADS * GLA_DV), F32),
        scratch_shapes=[state], compiler_params=_params(sem), name="gla_forward",
    )(proj, proj, proj, la_f)
    return pl.pallas_call(
        _gla_bwd_kernel, grid=grid,
        in_specs=specs(bwd_row) + [
            out_spec(bwd_row),
            pl.BlockSpec((GLA_STEP, GLA_DV), lambda b, h, n: (bwd_row(b, n), r_col + h)),
            pl.BlockSpec((1, GLA_DV), lambda b, h, n: (0, 0)),
        ],
        out_specs=out_spec(bwd_row),
        out_shape=jax.ShapeDtypeStruct((t, GLA_HEADS * GLA_DV), BF16),
        scratch_shapes=[state], compiler_params=_params(sem), name="gla_backward",
    )(proj, proj, proj, la_b, o_f, proj, norm_g.reshape(1, GLA_DV))


ROUTE_TILE = 512


def _route_kernel(lg_ref, bias_ref, idx_ref, w_ref, rank_ref, cnt_ref, run_ref, tri_ref):
    tm = ROUTE_TILE

    @pl.when(pl.program_id(0) == 0)
    def _():
        run_ref[...] = jnp.zeros_like(run_ref)
        r = lax.broadcasted_iota(I32, (tm, tm), 0)
        c = lax.broadcasted_iota(I32, (tm, tm), 1)
        tri_ref[...] = (r < c).astype(BF16)

    shape3 = (N_GROUPS, GROUP_SIZE, tm)
    scores = _sigmoid(lg_ref[...])
    biased = (scores + bias_ref[...]).reshape(shape3)
    scores3 = scores.reshape(shape3)
    in_grp = lax.broadcasted_iota(I32, shape3, 1)
    grp_id = lax.broadcasted_iota(I32, shape3, 0)
    exp_id = grp_id * GROUP_SIZE + in_grp

    m1 = jnp.max(biased, axis=1, keepdims=True)
    first = jnp.min(jnp.where(biased == m1, in_grp, GROUP_SIZE), axis=1, keepdims=True)
    m2 = jnp.max(jnp.where(in_grp == first, NEG_INF, biased), axis=1, keepdims=True)
    grp_score = m1 + m2

    g_iota = lax.broadcasted_iota(I32, (N_GROUPS, 1, tm), 0)
    g_sel = jnp.zeros((N_GROUPS, 1, tm), dtype=jnp.bool_)
    cur = grp_score
    for _ in range(TOPK_GROUPS):
        m = jnp.max(cur, axis=0, keepdims=True)
        pick = jnp.min(jnp.where(cur == m, g_iota, N_GROUPS), axis=0, keepdims=True)
        hit = g_iota == pick
        g_sel = g_sel | hit
        cur = jnp.where(hit, NEG_INF, cur)

    cand = jnp.where(g_sel, biased, NEG_INF)
    picks, weights = [], []
    sel = jnp.zeros(shape3, dtype=jnp.bool_)
    for _ in range(TOP_K):
        m = jnp.max(jnp.max(cand, axis=1, keepdims=True), axis=0, keepdims=True)
        pick = jnp.where(cand == m, exp_id, N_EXPERTS)
        pick = jnp.min(jnp.min(pick, axis=1, keepdims=True), axis=0, keepdims=True)
        hit = exp_id == pick
        sel = sel | hit
        cand = jnp.where(hit, NEG_INF, cand)
        w = jnp.where(hit, scores3, 0.0)
        weights.append(jnp.sum(jnp.sum(w, axis=1, keepdims=True), axis=0, keepdims=True))
        picks.append(pick)

    sel_f = jnp.where(sel, 1.0, 0.0).reshape(N_EXPERTS, tm)
    prefix = (_dot(sel_f.astype(BF16), tri_ref[...]) + run_ref[:, 0:1]).reshape(shape3)
    ranks = []
    for pick in picks:
        r = jnp.where(exp_id == pick, prefix, 0.0)
        ranks.append(jnp.sum(jnp.sum(r, axis=1, keepdims=True), axis=0, keepdims=True))
    new_run = run_ref[...] + jnp.sum(sel_f, axis=1, keepdims=True)
    run_ref[...] = new_run
    cnt_ref[...] = new_run.astype(I32)

    total = weights[0]
    for w in weights[1:]:
        total = total + w
    row = lax.broadcasted_iota(I32, (TOP_K, tm), 0)

    def stack(vals):
        out = jnp.zeros((TOP_K, tm), vals[0].dtype)
        for k, v in enumerate(vals):
            out = jnp.where(row == k, v.reshape(1, tm), out)
        return out

    idx_ref[...] = stack(picks)
    w_ref[...] = stack([w / total * ROUTED_SCALE for w in weights])
    rank_ref[...] = stack(ranks).astype(I32)


def _route(logits_t, bias):
    t = logits_t.shape[1]
    tm = ROUTE_TILE
    tok = pl.BlockSpec((TOP_K, tm), lambda i: (0, i))
    idx, w, rank, cnt = pl.pallas_call(
        _route_kernel,
        grid=(t // tm,),
        in_specs=[pl.BlockSpec((N_EXPERTS, tm), lambda i: (0, i)),
                  pl.BlockSpec((N_EXPERTS, 1), lambda i: (0, 0))],
        out_specs=[tok, tok, tok, pl.BlockSpec((N_EXPERTS, LANES), lambda i: (0, 0))],
        out_shape=[jax.ShapeDtypeStruct((TOP_K, t), I32),
                   jax.ShapeDtypeStruct((TOP_K, t), F32),
                   jax.ShapeDtypeStruct((TOP_K, t), I32),
                   jax.ShapeDtypeStruct((N_EXPERTS, LANES), I32)],
        scratch_shapes=[pltpu.VMEM((N_EXPERTS, LANES), F32), pltpu.VMEM((tm, tm), BF16)],
        compiler_params=_params(("arbitrary",)),
        name="moe_route",
    )(logits_t, bias.reshape(N_EXPERTS, 1))
    return idx, w, rank, cnt[:, 0]


DISPATCH_TILE = 512


def _token_rows(i):
    return pl.ds(pl.multiple_of(i * TOKEN_TILE_ROWS, TOKEN_TILE_ROWS), TOKEN_TILE_ROWS)


def _block_rows(b):
    n = SLOT_BLOCK * TOKEN_TILE_ROWS
    return pl.ds(pl.multiple_of(b * n, n), n)


def _dispatch_kernel(cnt_ref, start_ref, nu_ref, pos_ref, h3_ref, x3_ref, zero_ref, sem, pad_sem,
                     *, n_blocks):
    i = pl.program_id(0)
    tm = DISPATCH_TILE

    def issue(t, carry):
        for k in range(TOP_K):
            pltpu.make_async_copy(h3_ref.at[_token_rows(t)],
                                  x3_ref.at[_token_rows(pos_ref[k, t])], sem
                                  ).start(priority=k % N_DMA_PRIORITIES)
        return carry

    lax.fori_loop(0, tm, issue, 0)

    @pl.when(i == 0)
    def _():
        zero_ref[...] = jnp.zeros_like(zero_ref)
        zero_tile = zero_ref.at[pl.ds(0, TOKEN_TILE_ROWS)]

        def per_expert(e, n_pad):
            cnt = cnt_ref[e]
            padded = (cnt + SLOT_BLOCK - 1) // SLOT_BLOCK * SLOT_BLOCK

            def pad_row(r, c):
                pltpu.make_async_copy(zero_tile, x3_ref.at[_token_rows(start_ref[e] + r)],
                                      pad_sem).start()
                return c

            lax.fori_loop(cnt, padded, pad_row, 0)
            return n_pad + (padded - cnt)

        n_pad = lax.fori_loop(0, N_EXPERTS, per_expert, 0)

        def tail_block(b, c):
            pltpu.make_async_copy(zero_ref, x3_ref.at[_block_rows(b)], pad_sem).start()
            return c

        lax.fori_loop(nu_ref[0], n_blocks, tail_block, 0)

        def pad_wait(r, c):
            pltpu.make_async_copy(zero_tile, x3_ref.at[_token_rows(0)], pad_sem).wait()
            return c

        lax.fori_loop(0, n_pad, pad_wait, 0)

        def tail_wait(b, c):
            pltpu.make_async_copy(zero_ref, x3_ref.at[_block_rows(0)], pad_sem).wait()
            return c

        lax.fori_loop(nu_ref[0], n_blocks, tail_wait, 0)

    n_rows = tm * TOP_K * TOKEN_TILE_ROWS
    pltpu.make_async_copy(x3_ref.at[pl.ds(0, n_rows)], x3_ref.at[pl.ds(0, n_rows)], sem).wait()


def _dispatch(h3, pos, counts, starts, n_used, n_blocks):
    t = pos.shape[1]
    tm = DISPATCH_TILE
    return pl.pallas_call(
        functools.partial(_dispatch_kernel, n_blocks=n_blocks),
        grid_spec=pltpu.PrefetchScalarGridSpec(
            num_scalar_prefetch=3,
            grid=(t // tm,),
            in_specs=[pl.BlockSpec((TOP_K, tm), lambda i, c, s, u: (0, i), memory_space=pltpu.SMEM),
                      pl.BlockSpec((tm * TOKEN_TILE_ROWS, LANES), lambda i, c, s, u: (i, 0))],
            out_specs=pl.BlockSpec(memory_space=pl.ANY),
            scratch_shapes=[pltpu.VMEM((SLOT_BLOCK * TOKEN_TILE_ROWS, LANES), I32),
                            pltpu.SemaphoreType.DMA, pltpu.SemaphoreType.DMA],
        ),
        out_shape=jax.ShapeDtypeStruct((n_blocks * SLOT_BLOCK * TOKEN_TILE_ROWS, LANES), I32),
        compiler_params=_params(("arbitrary",)),
        name="moe_dispatch",
    )(counts, starts, n_used, pos, h3)


def _expert_kernel(be_ref, nu_ref, x3_ref, wg_ref, wu_ref, wd_ref, y3_ref):
    @pl.when(pl.program_id(0) >= nu_ref[0])
    def _():
        y3_ref[...] = jnp.zeros_like(y3_ref)

    @pl.when(pl.program_id(0) < nu_ref[0])
    def _():
        m = SLOT_BLOCK
        pieces = []
        for c in range(TOKEN_TILE_ROWS):
            lo, hi = _unpack_pair(x3_ref[pl.ds(c, m, stride=TOKEN_TILE_ROWS), :])
            pieces += [lo.astype(BF16), hi.astype(BF16)]
        x = jnp.concatenate(pieces, axis=1)
        a = (_silu(_dot(x, wg_ref[0, 0])) * _dot(x, wu_ref[0, 0])).astype(BF16)
        y = _dot(a, wd_ref[0, 0])
        for c in range(TOKEN_TILE_ROWS):
            base = 2 * LANES * c
            y3_ref[pl.ds(c, m, stride=TOKEN_TILE_ROWS), :] = _pack_pair(
                y[:, base:base + LANES], y[:, base + LANES:base + 2 * LANES])


def _experts(x3, block_expert, n_used, layer, w_gate, w_up, w_down):
    n_blocks = block_expert.shape[0]
    rows = SLOT_BLOCK * TOKEN_TILE_ROWS

    def blk(b, be, nu):
        return jnp.minimum(b, nu[0] - 1)

    def expert_block(b, be, nu):
        return (layer, be[blk(b, be, nu)], 0, 0)

    return pl.pallas_call(
        _expert_kernel,
        grid_spec=pltpu.PrefetchScalarGridSpec(
            num_scalar_prefetch=2,
            grid=(n_blocks,),
            in_specs=[
                pl.BlockSpec((rows, LANES), lambda b, be, nu: (blk(b, be, nu), 0)),
                pl.BlockSpec((1, 1, D_MODEL, EXPERT_DIM), expert_block),
                pl.BlockSpec((1, 1, D_MODEL, EXPERT_DIM), expert_block),
                pl.BlockSpec((1, 1, EXPERT_DIM, D_MODEL), expert_block),
            ],
            out_specs=pl.BlockSpec((rows, LANES), lambda b, be, nu: (b, 0)),
        ),
        out_shape=jax.ShapeDtypeStruct(x3.shape, I32),
        compiler_params=_params(("arbitrary",)),
        name="moe_experts",
    )(block_expert, n_used, x3, w_gate, w_up, w_down)


COMBINE_TILE = 256


def _combine_kernel(pos_ref, x_ref, h_ref, w_ref, gate_ref, sg_ref, su_ref, sd_ref, y3_ref,
                    o_ref, buf_ref, sem):
    tm = COMBINE_TILE

    def issue(t, carry):
        for k in range(TOP_K):
            pltpu.make_async_copy(y3_ref.at[_token_rows(pos_ref[k, t])],
                                  buf_ref.at[_token_rows(k * tm + t)], sem
                                  ).start(priority=k % N_DMA_PRIORITIES)
        return carry

    lax.fori_loop(0, tm, issue, 0)

    h = h_ref[...]
    a = (_silu(_dot(h, sg_ref[...])) * _dot(h, su_ref[...])).astype(BF16)
    shared = _dot(a, sd_ref[...])

    pltpu.make_async_copy(y3_ref.at[pl.ds(0, buf_ref.shape[0])], buf_ref, sem).wait()

    gate = gate_ref[0]
    wk = [w_ref[:, k:k + 1] for k in range(TOP_K)]
    for c in range(TOKEN_TILE_ROWS):
        lo_acc = None
        for k in range(TOP_K):
            word = buf_ref[pl.ds(k * tm * TOKEN_TILE_ROWS + c, tm, stride=TOKEN_TILE_ROWS), :]
            lo, hi = _unpack_pair(word)
            if lo_acc is None:
                lo_acc, hi_acc = wk[k] * lo, wk[k] * hi
            else:
                lo_acc, hi_acc = lo_acc + wk[k] * lo, hi_acc + wk[k] * hi
        base = 2 * LANES * c
        for off, acc in ((base, lo_acc), (base + LANES, hi_acc)):
            cols = slice(off, off + LANES)
            o_ref[:, cols] = x_ref[:, cols] + gate[:, cols] * (acc + shared[:, cols])


def _combine(x, h, y3, pos, w_tok, mod3, gate_slot, seq_len, s_gate, s_up, s_down):
    t = x.shape[0]
    tm = COMBINE_TILE
    sps = seq_len // tm
    full = lambda shape: pl.BlockSpec(shape, lambda i: tuple(0 for _ in shape))
    return pl.pallas_call(
        _combine_kernel,
        grid=(t // tm,),
        in_specs=[
            pl.BlockSpec((TOP_K, tm), lambda i: (0, i), memory_space=pltpu.SMEM),
            pl.BlockSpec((tm, D_MODEL), lambda i: (i, 0)),
            pl.BlockSpec((tm, D_MODEL), lambda i: (i, 0)),
            pl.BlockSpec((tm, TOP_K), lambda i: (i, 0)),
            _mod_spec(gate_slot, sps),
            full((D_MODEL, EXPERT_DIM)), full((D_MODEL, EXPERT_DIM)), full((EXPERT_DIM, D_MODEL)),
            pl.BlockSpec(memory_space=pl.ANY),
        ],
        out_specs=pl.BlockSpec((tm, D_MODEL), lambda i: (i, 0)),
        out_shape=jax.ShapeDtypeStruct((t, D_MODEL), F32),
        scratch_shapes=[pltpu.VMEM((TOP_K * tm * TOKEN_TILE_ROWS, LANES), I32),
                        pltpu.SemaphoreType.DMA],
        compiler_params=_params(("arbitrary",)),
        name="moe_combine",
    )(pos, x, h, w_tok, mod3, s_gate, s_up, s_down, y3)


def _moe_layer(x, mod3, norm_g, seq_len, layer, router_wt, router_bias, w_gate, w_up, w_down,
               s_gate, s_up, s_down):
    t = x.shape[0]
    h, h3, logits_t = _norm_mod(x, norm_g, mod3, 3, 4, seq_len, router_wt)
    idx, w, rank, counts = _route(logits_t, router_bias)
    n_blocks = -(-(t * TOP_K + N_EXPERTS * (SLOT_BLOCK - 1)) // SLOT_BLOCK)
    padded = (counts + SLOT_BLOCK - 1) // SLOT_BLOCK * SLOT_BLOCK
    padded_end = jnp.cumsum(padded)
    starts = (padded_end - padded).astype(I32)
    block_start = jnp.arange(n_blocks, dtype=I32) * SLOT_BLOCK
    block_expert = jnp.minimum(
        jnp.sum((padded_end[None, :] <= block_start[:, None]).astype(I32), axis=1), N_EXPERTS - 1)
    n_used = (padded_end[-1:] // SLOT_BLOCK).astype(I32)
    experts = jnp.arange(N_EXPERTS, dtype=I32)
    pos = jnp.sum(jnp.where(idx[:, :, None] == experts, starts, 0), axis=-1) + rank
    x3 = _dispatch(h3, pos, counts.astype(I32), starts, n_used, n_blocks)
    y3 = _experts(x3, block_expert, n_used, layer, w_gate, w_up, w_down)
    return _combine(x, h, y3, pos, w.T, mod3, 5, seq_len, s_gate, s_up, s_down)


def _trunk(x, mod, batch, seq_len, p):
    t = batch * seq_len
    x = x.reshape(t, D_MODEL)
    for layer in range(DEPTH):
        mod3 = mod[layer].reshape(batch * 6, 1, D_MODEL)
        g_mix = p["norm_mix"][layer].reshape(1, D_MODEL)
        g_ffn = p["norm_ffn"][layer].reshape(1, D_MODEL)
        h = _norm_mod(x, g_mix, mod3, 0, 1, seq_len)
        if layer % 2 == 0:
            a = layer // 2
            qkv = _matmul(h, p["attn_w_in"][a])
            o = _attention(qkv, p["attn_q_norm"][a], p["attn_k_norm"][a], p["attn_sink"][a],
                           batch, seq_len)
            x = _matmul_residual(o, p["attn_w_out"][a], x, mod3, 2, seq_len)
        else:
            g = layer // 2
            proj = _matmul(h, p["gla_w_main"][g])
            la_f, la_b = _gla_gates(h, p["gla_w_lr"][g], p["gla_w_g"][g], p["gla_b_g"][g])
            o = _gla(proj, la_f, la_b, p["gla_norm"][g], batch, seq_len)
            x = _matmul_residual(o, p["gla_w_out"][g], x, mod3, 2, seq_len)
        x = _moe_layer(x, mod3, g_ffn, seq_len, layer, p["router_wt"][layer], p["router_bias"][layer],
                       p["exp_w_gate"], p["exp_w_up"], p["exp_w_down"],
                       p["sh_w_gate"][layer], p["sh_w_up"][layer], p["sh_w_down"][layer])
    return x.reshape(batch, seq_len, D_MODEL)


def kernel(x_prompt, x_sample, c_prompt, c_sample, ada_w, ada_b, norm_mix, norm_ffn, attn_w_in, attn_q_norm, attn_k_norm, attn_sink, attn_w_out, gla_w_in, gla_w_g2_f, gla_b_g_f, gla_w_g2_b, gla_b_g_b, gla_norm, gla_w_out, router_w, router_bias, exp_w_gate, exp_w_up, exp_w_down, sh_w_gate, sh_w_up, sh_w_down):
    bp, sp, _ = x_prompt.shape
    bs, ss, _ = x_sample.shape
    n_gate = GLA_HEADS * GLA_DK
    n_layers_gla = gla_w_in.shape[0]

    w_lr = jnp.zeros((n_layers_gla, D_MODEL, GLA_GATE_PAD), F32)
    w_lr = w_lr.at[:, :, :2 * GLA_RANK].set(gla_w_in[:, :, GLA_MAIN:])
    w_g = jnp.zeros((n_layers_gla, GLA_GATE_PAD, 2 * n_gate), F32)
    w_g = w_g.at[:, :GLA_RANK, :n_gate].set(gla_w_g2_f)
    w_g = w_g.at[:, GLA_RANK:2 * GLA_RANK, n_gate:].set(gla_w_g2_b)
    p = {
        "norm_mix": norm_mix, "norm_ffn": norm_ffn,
        "attn_w_in": attn_w_in.astype(BF16), "attn_q_norm": attn_q_norm,
        "attn_k_norm": attn_k_norm, "attn_sink": attn_sink, "attn_w_out": attn_w_out.astype(BF16),
        "gla_w_main": gla_w_in[:, :, :GLA_MAIN].astype(BF16), "gla_w_lr": w_lr, "gla_w_g": w_g,
        "gla_b_g": jnp.concatenate([gla_b_g_f, gla_b_g_b], axis=-1).reshape(n_layers_gla, 1, 2 * n_gate),
        "gla_norm": gla_norm, "gla_w_out": gla_w_out.astype(BF16),
        "router_wt": jnp.swapaxes(router_w, 1, 2), "router_bias": router_bias,
        "exp_w_gate": exp_w_gate.astype(BF16), "exp_w_up": exp_w_up.astype(BF16),
        "exp_w_down": exp_w_down.astype(BF16),
        "sh_w_gate": sh_w_gate.astype(BF16), "sh_w_up": sh_w_up.astype(BF16),
        "sh_w_down": sh_w_down.astype(BF16),
    }

    c_all = jnp.zeros((SUBLANES, D_MODEL), F32).at[:bp + bs].set(jnp.concatenate([c_prompt, c_sample], 0))
    mod = _modulation(c_all, ada_w, ada_b)
    y_prompt = _trunk(x_prompt, mod[:, :bp], bp, sp, p)
    y_sample = _trunk(x_sample, mod[:, bp:bp + bs], bs, ss, p)
    return (y_prompt, y_sample)
```

```python
import functools

import numpy as np
import jax
import jax.numpy as jnp
from jax import lax
from jax.experimental import pallas as pl
from jax.experimental.pallas import tpu as pltpu

F32 = jnp.float32
BF16 = jnp.bfloat16
I32 = jnp.int32

D_MODEL = 2048
DEPTH = 2
EPS = 1e-6
HEAD_DIM = 128
N_Q_HEADS = 16
N_KV_HEADS = 4
Q_PER_KV = 4
WINDOW = 128
ATTN_BLOCK = 128
GLA_HEADS = 4
GLA_DK = 256
GLA_DV = 512
GLA_RANK = 16
GLA_TAU = 16.0
GLA_CHUNK = 64
GLA_MAIN = 2 * GLA_HEADS * GLA_DK + 2 * GLA_HEADS * GLA_DV
N_EXPERTS = 64
TOP_K = 8
N_GROUPS = 8
GROUP_SIZE = N_EXPERTS // N_GROUPS
TOPK_GROUPS = 4
EXPERT_DIM = 512
ROUTED_SCALE = 2.5
SLOT_BLOCK = 256
EXPERT_STEP_BLOCKS = 2

SUBLANES = 8
LANES = 128
TOKEN_TILE_ROWS = D_MODEL // (2 * LANES)
VMEM_LIMIT_BYTES = 56 * 1024 * 1024
N_DMA_PRIORITIES = 2

NEG_INF = float("-inf")


def _params(semantics):
    return pltpu.CompilerParams(dimension_semantics=semantics, vmem_limit_bytes=VMEM_LIMIT_BYTES)


def _dot(a, b):
    return jnp.dot(a, b, preferred_element_type=F32)


def _dot_nt(a, b):
    return lax.dot_general(a, b, (((1,), (1,)), ((), ())), preferred_element_type=F32)


def _dot_tn(a, b):
    return lax.dot_general(a, b, (((0,), (0,)), ((), ())), preferred_element_type=F32)


def _split3(x):
    hi = x.astype(BF16)
    r1 = x - hi.astype(F32)
    mid = r1.astype(BF16)
    lo = (r1 - mid.astype(F32)).astype(BF16)
    return hi, mid, lo


def _sigmoid(x):
    return 1.0 / (1.0 + jnp.exp(-x))


def _silu(x):
    return x * _sigmoid(x)


def _pack_pair(lo, hi):
    lo_bits = lax.bitcast_convert_type(lo.astype(BF16).astype(F32), I32)
    hi_bits = lax.bitcast_convert_type(hi.astype(BF16).astype(F32), I32)
    return lax.shift_right_logical(lo_bits, 16) | (hi_bits & jnp.int32(-65536))


def _unpack_pair(word):
    lo = lax.bitcast_convert_type(lax.shift_left(word, 16), F32)
    hi = lax.bitcast_convert_type(word & jnp.int32(-65536), F32)
    return lo, hi


def _mod_kernel(c_ref, w_ref, b_ref, o_ref):
    a = _silu(c_ref[...])
    o_ref[0] = jnp.dot(a, w_ref[0], preferred_element_type=F32,
                       precision=lax.Precision.HIGHEST) + b_ref[0]


def _modulation(c_all, ada_w, ada_b):
    n = 6 * D_MODEL
    tn = 1024
    return pl.pallas_call(
        _mod_kernel,
        grid=(DEPTH, n // tn),
        in_specs=[
            pl.BlockSpec((SUBLANES, D_MODEL), lambda l, j: (0, 0)),
            pl.BlockSpec((1, D_MODEL, tn), lambda l, j: (l, 0, j)),
            pl.BlockSpec((1, 1, tn), lambda l, j: (l, 0, j)),
        ],
        out_specs=pl.BlockSpec((1, SUBLANES, tn), lambda l, j: (l, 0, j)),
        out_shape=jax.ShapeDtypeStruct((DEPTH, SUBLANES, n), F32),
        compiler_params=_params(("arbitrary", "arbitrary")),
        name="adaln_mod",
    )(c_all, ada_w, ada_b.reshape(DEPTH, 1, n))


def _normed(x_ref, g_ref, sc_ref, sh_ref):
    x = x_ref[...]
    ms = jnp.mean(x * x, axis=-1, keepdims=True)
    y = x * lax.rsqrt(ms + EPS) * g_ref[...]
    return y * (1.0 + sc_ref[0]) + sh_ref[0]


def _norm_kernel(x_ref, g_ref, sc_ref, sh_ref, h_ref):
    h_ref[...] = _normed(x_ref, g_ref, sc_ref, sh_ref).astype(BF16)


def _norm_router_kernel(x_ref, g_ref, sc_ref, sh_ref, wr_ref, h_ref, h3_ref, lg_ref):
    h = _normed(x_ref, g_ref, sc_ref, sh_ref)
    tm = h.shape[0]
    h_hi = h.astype(BF16)
    h_ref[...] = h_hi
    h_lo = (h - h_hi.astype(F32)).astype(BF16)
    w_hi, w_mid, _ = _split3(wr_ref[...])
    lg_ref[...] = _dot_nt(w_hi, h_hi) + _dot_nt(w_hi, h_lo) + _dot_nt(w_mid, h_hi)
    h_r = h_hi.astype(F32)
    for c in range(TOKEN_TILE_ROWS):
        base = 2 * LANES * c
        word = _pack_pair(h_r[:, base:base + LANES], h_r[:, base + LANES:base + 2 * LANES])
        h3_ref[pl.ds(c, tm, stride=TOKEN_TILE_ROWS), :] = word


def _mod_spec(slot, steps_per_seq, width=D_MODEL):
    return pl.BlockSpec((1, 1, width), lambda i: (6 * (i // steps_per_seq) + slot, 0, 0))


def _norm_mod(x, g, mod3, shift_slot, scale_slot, seq_len, router_wt=None):
    t = x.shape[0]
    tm = 512
    sps = seq_len // tm
    in_specs = [
        pl.BlockSpec((tm, D_MODEL), lambda i: (i, 0)),
        pl.BlockSpec((1, D_MODEL), lambda i: (0, 0)),
        _mod_spec(scale_slot, sps),
        _mod_spec(shift_slot, sps),
    ]
    h_spec = pl.BlockSpec((tm, D_MODEL), lambda i: (i, 0))
    h_shape = jax.ShapeDtypeStruct((t, D_MODEL), BF16)
    if router_wt is None:
        return pl.pallas_call(
            _norm_kernel, grid=(t // tm,), in_specs=in_specs, out_specs=h_spec, out_shape=h_shape,
            compiler_params=_params(("arbitrary",)), name="norm_mod",
        )(x, g, mod3, mod3)
    return pl.pallas_call(
        _norm_router_kernel,
        grid=(t // tm,),
        in_specs=in_specs + [pl.BlockSpec((N_EXPERTS, D_MODEL), lambda i: (0, 0))],
        out_specs=[
            h_spec,
            pl.BlockSpec((tm * TOKEN_TILE_ROWS, LANES), lambda i: (i, 0)),
            pl.BlockSpec((N_EXPERTS, tm), lambda i: (0, i)),
        ],
        out_shape=[
            h_shape,
            jax.ShapeDtypeStruct((t * TOKEN_TILE_ROWS, LANES), I32),
            jax.ShapeDtypeStruct((N_EXPERTS, t), F32),
        ],
        compiler_params=_params(("arbitrary",)),
        name="norm_mod_router",
    )(x, g, mod3, mod3, router_wt)


def _mm_kernel(a_ref, w_ref, o_ref):
    o_ref[...] = _dot(a_ref[...], w_ref[...]).astype(o_ref.dtype)


def _mm_res_kernel(a_ref, w_ref, x_ref, gate_ref, o_ref):
    o_ref[...] = x_ref[...] + gate_ref[0] * _dot(a_ref[...], w_ref[...])


def _matmul(a, w, out_dtype=BF16, tm=1024, tn=512):
    m, k = a.shape
    n = w.shape[1]
    return pl.pallas_call(
        _mm_kernel,
        grid=(m // tm, n // tn),
        in_specs=[pl.BlockSpec((tm, k), lambda i, j: (i, 0)),
                  pl.BlockSpec((k, tn), lambda i, j: (0, j))],
        out_specs=pl.BlockSpec((tm, tn), lambda i, j: (i, j)),
        out_shape=jax.ShapeDtypeStruct((m, n), out_dtype),
        compiler_params=_params(("arbitrary", "arbitrary")),
        name="matmul",
    )(a, w)


def _matmul_residual(a, w, x, mod3, gate_slot, seq_len, tm=1024, tn=512):
    m, k = a.shape
    n = w.shape[1]
    sps = seq_len // tm
    return pl.pallas_call(
        _mm_res_kernel,
        grid=(m // tm, n // tn),
        in_specs=[pl.BlockSpec((tm, k), lambda i, j: (i, 0)),
                  pl.BlockSpec((k, tn), lambda i, j: (0, j)),
                  pl.BlockSpec((tm, tn), lambda i, j: (i, j)),
                  pl.BlockSpec((1, 1, tn), lambda i, j: (6 * (i // sps) + gate_slot, 0, j))],
        out_specs=pl.BlockSpec((tm, tn), lambda i, j: (i, j)),
        out_shape=jax.ShapeDtypeStruct((m, n), F32),
        compiler_params=_params(("arbitrary", "arbitrary")),
        name="matmul_residual",
    )(a, w, x, mod3)


_ALIBI_SLOPES = [float(s) for s in
                 np.exp2(-8.0 * np.arange(1, N_Q_HEADS + 1, dtype=np.float32) / N_Q_HEADS)]


def _head_norm(x, gain):
    xf = x.astype(F32)
    ms = jnp.mean(xf * xf, axis=-1, keepdims=True)
    return xf * lax.rsqrt(ms + EPS) * gain


def _attn_kernel(sink_ref, q_ref, kp_ref, kc_ref, kn_ref, vp_ref, vc_ref, vn_ref,
                 qg_ref, kg_ref, o_ref, *, n_blocks):
    i = pl.program_id(1)
    span = ATTN_BLOCK + 2 * WINDOW
    t_idx = lax.broadcasted_iota(I32, (ATTN_BLOCK, span), 0)
    j_idx = lax.broadcasted_iota(I32, (ATTN_BLOCK, span), 1)
    dist = jnp.abs(t_idx - j_idx + WINDOW)
    valid = dist <= WINDOW
    valid = valid & ((j_idx >= WINDOW) | (i > 0))
    valid = valid & ((j_idx < WINDOW + ATTN_BLOCK) | (i < n_blocks - 1))
    dist_f = dist.astype(F32)
    q_gain = qg_ref[...] * (HEAD_DIM ** -0.5)
    k_gain = kg_ref[...]
    for g in range(N_KV_HEADS):
        cols = slice(g * HEAD_DIM, (g + 1) * HEAD_DIM)
        k_all = jnp.concatenate([kp_ref[:, cols], kc_ref[:, cols], kn_ref[:, cols]], axis=0)
        v_all = jnp.concatenate([vp_ref[:, cols], vc_ref[:, cols], vn_ref[:, cols]], axis=0)
        k_n = _head_norm(k_all, k_gain).astype(BF16)
        q_n = []
        for jh in range(Q_PER_KV):
            h = g * Q_PER_KV + jh
            q_n.append(_head_norm(q_ref[:, h * HEAD_DIM:(h + 1) * HEAD_DIM], q_gain).astype(BF16))
        s_all = _dot_nt(jnp.concatenate(q_n, axis=0), k_n)
        p_all, inv_den = [], []
        for jh in range(Q_PER_KV):
            h = g * Q_PER_KV + jh
            sink = sink_ref[h]
            s = s_all[jh * ATTN_BLOCK:(jh + 1) * ATTN_BLOCK] - _ALIBI_SLOPES[h] * dist_f
            s = jnp.where(valid, s, NEG_INF)
            m = jnp.maximum(jnp.max(s, axis=-1, keepdims=True), sink)
            p = jnp.exp(s - m)
            den = jnp.sum(p, axis=-1, keepdims=True) + jnp.exp(sink - m)
            p_all.append(p.astype(BF16))
            inv_den.append(1.0 / den)
        o_all = _dot(jnp.concatenate(p_all, axis=0), v_all)
        outs = [o_all[jh * ATTN_BLOCK:(jh + 1) * ATTN_BLOCK] * inv_den[jh] for jh in range(Q_PER_KV)]
        o_ref[:, g * Q_PER_KV * HEAD_DIM:(g + 1) * Q_PER_KV * HEAD_DIM] = (
            jnp.concatenate(outs, axis=1).astype(BF16))


def _attention(qkv, q_gain, k_gain, sink, batch, seq_len):
    t = qkv.shape[0]
    nb = seq_len // ATTN_BLOCK
    qw = N_Q_HEADS * HEAD_DIM
    kvw = N_KV_HEADS * HEAD_DIM
    k_col = qw // kvw
    v_col = k_col + 1

    def rows(b, i):
        return b * nb + i

    def kv_spec(col, shift):
        return pl.BlockSpec(
            (ATTN_BLOCK, kvw),
            lambda b, i: (rows(b, jnp.clip(i + shift, 0, nb - 1)), col))

    return pl.pallas_call(
        functools.partial(_attn_kernel, n_blocks=nb),
        grid=(batch, nb),
        in_specs=[
            pl.BlockSpec(memory_space=pltpu.SMEM),
            pl.BlockSpec((ATTN_BLOCK, qw), lambda b, i: (rows(b, i), 0)),
            kv_spec(k_col, -1), kv_spec(k_col, 0), kv_spec(k_col, 1),
            kv_spec(v_col, -1), kv_spec(v_col, 0), kv_spec(v_col, 1),
            pl.BlockSpec((1, HEAD_DIM), lambda b, i: (0, 0)),
            pl.BlockSpec((1, HEAD_DIM), lambda b, i: (0, 0)),
        ],
        out_specs=pl.BlockSpec((ATTN_BLOCK, qw), lambda b, i: (rows(b, i), 0)),
        out_shape=jax.ShapeDtypeStruct((t, qw), BF16),
        compiler_params=_params(("arbitrary", "arbitrary")),
        name="window_attention",
    )(sink, qkv, qkv, qkv, qkv, qkv, qkv, qkv, q_gain.reshape(1, HEAD_DIM), k_gain.reshape(1, HEAD_DIM))


GLA_GATE_PAD = LANES


def _gla_gate_kernel(h_ref, wlr_ref, wg_ref, bg_ref, laf_ref, lab_ref):
    h = h_ref[...]
    w_hi, w_mid, _ = _split3(wlr_ref[...])
    lr = _dot(h, w_hi) + _dot(h, w_mid)
    l_hi, l_mid, l_lo = _split3(lr)
    g_hi, g_mid, _ = _split3(wg_ref[...])
    z = (_dot(l_hi, g_hi) + _dot(l_mid, g_hi) + _dot(l_hi, g_mid) + _dot(l_lo, g_hi)
         + bg_ref[...])
    log_sig = jnp.minimum(z, 0.0) - jnp.log(1.0 + jnp.exp(-jnp.abs(z)))
    la = log_sig / GLA_TAU
    n = GLA_HEADS * GLA_DK
    laf_ref[...] = la[:, :n]
    lab_ref[...] = la[:, n:]


def _gla_gates(h, w_lr, w_g, b_g):
    t = h.shape[0]
    tm = 512
    n = GLA_HEADS * GLA_DK
    out = jax.ShapeDtypeStruct((t, n), F32)
    return pl.pallas_call(
        _gla_gate_kernel,
        grid=(t // tm,),
        in_specs=[pl.BlockSpec((tm, D_MODEL), lambda i: (i, 0)),
                  pl.BlockSpec((D_MODEL, GLA_GATE_PAD), lambda i: (0, 0)),
                  pl.BlockSpec((GLA_GATE_PAD, 2 * n), lambda i: (0, 0)),
                  pl.BlockSpec((1, 2 * n), lambda i: (0, 0))],
        out_specs=[pl.BlockSpec((tm, n), lambda i: (i, 0)), pl.BlockSpec((tm, n), lambda i: (i, 0))],
        out_shape=[out, out],
        compiler_params=_params(("arbitrary",)),
        name="gla_gates",
    )(h, w_lr, w_g, b_g)


GLA_STEP = 256


def _gla_chunks(q_ref, k_ref, v_ref, la_ref, st_ref, reverse):
    shift = GLA_CHUNK.bit_length() - 1
    r_idx = lax.broadcasted_iota(I32, (GLA_STEP, GLA_STEP), 0)
    c_idx = lax.broadcasted_iota(I32, (GLA_STEP, GLA_STEP), 1)
    same_chunk = lax.shift_right_logical(r_idx, shift) == lax.shift_right_logical(c_idx, shift)
    if reverse:
        sum_mat = same_chunk & (c_idx >= r_idx)
        keep = same_chunk & (c_idx > r_idx)
    else:
        sum_mat = same_chunk & (c_idx <= r_idx)
        keep = sum_mat
    sum_mat = jnp.where(sum_mat, 1.0, 0.0).astype(BF16)
    chunk_mat = jnp.where(same_chunk, 1.0, 0.0).astype(BF16)
    n_chunks = GLA_STEP // GLA_CHUNK
    la_hi, la_mid, la_lo = _split3(la_ref[...])
    b = _dot(sum_mat, la_hi) + _dot(sum_mat, la_mid) + _dot(sum_mat, la_lo)
    total = _dot(chunk_mat, la_hi) + _dot(chunk_mat, la_mid) + _dot(chunk_mat, la_lo)
    q = q_ref[...].astype(F32) * (GLA_DK ** -0.5)
    k = k_ref[...].astype(F32)
    v = v_ref[...]
    q_dec = (q * jnp.exp(b)).astype(BF16)
    k_inv = (k * jnp.exp(-b)).astype(BF16)
    k_end = (k * jnp.exp(total - b)).astype(BF16)
    decay = jnp.exp(total)
    a = jnp.where(keep, _dot_nt(q_dec, k_inv), 0.0).astype(BF16)
    o_intra = _dot(a, v)
    order = range(n_chunks - 1, -1, -1) if reverse else range(n_chunks)
    outs = [None] * n_chunks
    st = st_ref[...]
    for c in order:
        lo = c * GLA_CHUNK
        rows = slice(lo, lo + GLA_CHUNK)
        outs[c] = _dot_nt(q_dec[rows], st.astype(BF16))
        st = st * decay[lo:lo + 1, :] + _dot_tn(v[rows], k_end[rows])
    st_ref[...] = st
    return o_intra + jnp.concatenate(outs, axis=0)


def _gla_fwd_kernel(q_ref, k_ref, v_ref, la_ref, o_ref, st_ref):
    @pl.when(pl.program_id(2) == 0)
    def _():
        st_ref[...] = jnp.zeros_like(st_ref)
    o_ref[...] = _gla_chunks(q_ref, k_ref, v_ref, la_ref, st_ref, reverse=False)


def _gla_bwd_kernel(q_ref, k_ref, v_ref, la_ref, of_ref, r_ref, ng_ref, o_ref, st_ref):
    @pl.when(pl.program_id(2) == 0)
    def _():
        st_ref[...] = jnp.zeros_like(st_ref)
    o = of_ref[...] + _gla_chunks(q_ref, k_ref, v_ref, la_ref, st_ref, reverse=True)
    ms = jnp.mean(o * o, axis=-1, keepdims=True)
    o = o * lax.rsqrt(ms + EPS) * ng_ref[...]
    o_ref[...] = (o * _silu(r_ref[...].astype(F32))).astype(BF16)


def _gla(proj, la_f, la_b, norm_g, batch, seq_len):
    t = proj.shape[0]
    nblk = seq_len // GLA_STEP
    k_col = GLA_HEADS
    v_col = 2 * GLA_HEADS * GLA_DK // GLA_DV
    r_col = v_col + GLA_HEADS

    def specs(row):
        return [
            pl.BlockSpec((GLA_STEP, GLA_DK), lambda b, h, n: (row(b, n), h)),
            pl.BlockSpec((GLA_STEP, GLA_DK), lambda b, h, n: (row(b, n), k_col + h)),
            pl.BlockSpec((GLA_STEP, GLA_DV), lambda b, h, n: (row(b, n), v_col + h)),
            pl.BlockSpec((GLA_STEP, GLA_DK), lambda b, h, n: (row(b, n), h)),
        ]

    def out_spec(row):
        return pl.BlockSpec((GLA_STEP, GLA_DV), lambda b, h, n: (row(b, n), h))

    def fwd_row(b, n):
        return b * nblk + n

    def bwd_row(b, n):
        return b * nblk + (nblk - 1 - n)

    grid = (batch, GLA_HEADS, nblk)
    state = pltpu.VMEM((GLA_DV, GLA_DK), F32)
    sem = ("arbitrary", "arbitrary", "arbitrary")
    o_f = pl.pallas_call(
        _gla_fwd_kernel, grid=grid, in_specs=specs(fwd_row), out_specs=out_spec(fwd_row),
        out_shape=jax.ShapeDtypeStruct((t, GLA_HEADS * GLA_DV), F32),
        scratch_shapes=[state], compiler_params=_params(sem), name="gla_forward",
    )(proj, proj, proj, la_f)
    return pl.pallas_call(
        _gla_bwd_kernel, grid=grid,
        in_specs=specs(bwd_row) + [
            out_spec(bwd_row),
            pl.BlockSpec((GLA_STEP, GLA_DV), lambda b, h, n: (bwd_row(b, n), r_col + h)),
            pl.BlockSpec((1, GLA_DV), lambda b, h, n: (0, 0)),
        ],
        out_specs=out_spec(bwd_row),
        out_shape=jax.ShapeDtypeStruct((t, GLA_HEADS * GLA_DV), BF16),
        scratch_shapes=[state], compiler_params=_params(sem), name="gla_backward",
    )(proj, proj, proj, la_b, o_f, proj, norm_g.reshape(1, GLA_DV))


ROUTE_TILE = 512


def _route_kernel(lg_ref, bias_ref, idx_ref, w_ref, rank_ref, cnt_ref, run_ref, tri_ref):
    tm = ROUTE_TILE

    @pl.when(pl.program_id(0) == 0)
    def _():
        run_ref[...] = jnp.zeros_like(run_ref)
        r = lax.broadcasted_iota(I32, (tm, tm), 0)
        c = lax.broadcasted_iota(I32, (tm, tm), 1)
        tri_ref[...] = (r < c).astype(BF16)

    shape3 = (N_GROUPS, GROUP_SIZE, tm)
    scores = _sigmoid(lg_ref[...])
    biased = (scores + bias_ref[...]).reshape(shape3)
    scores3 = scores.reshape(shape3)
    in_grp = lax.broadcasted_iota(I32, shape3, 1)
    grp_id = lax.broadcasted_iota(I32, shape3, 0)
    exp_id = grp_id * GROUP_SIZE + in_grp

    m1 = jnp.max(biased, axis=1, keepdims=True)
    first = jnp.min(jnp.where(biased == m1, in_grp, GROUP_SIZE), axis=1, keepdims=True)
    m2 = jnp.max(jnp.where(in_grp == first, NEG_INF, biased), axis=1, keepdims=True)
    grp_score = m1 + m2

    g_iota = lax.broadcasted_iota(I32, (N_GROUPS, 1, tm), 0)
    g_sel = jnp.zeros((N_GROUPS, 1, tm), dtype=jnp.bool_)
    cur = grp_score
    for _ in range(TOPK_GROUPS):
        m = jnp.max(cur, axis=0, keepdims=True)
        pick = jnp.min(jnp.where(cur == m, g_iota, N_GROUPS), axis=0, keepdims=True)
        hit = g_iota == pick
        g_sel = g_sel | hit
        cur = jnp.where(hit, NEG_INF, cur)

    cand = jnp.where(g_sel, biased, NEG_INF)
    picks, weights = [], []
    sel = jnp.zeros(shape3, dtype=jnp.bool_)
    for _ in range(TOP_K):
        m = jnp.max(jnp.max(cand, axis=1, keepdims=True), axis=0, keepdims=True)
        pick = jnp.where(cand == m, exp_id, N_EXPERTS)
        pick = jnp.min(jnp.min(pick, axis=1, keepdims=True), axis=0, keepdims=True)
        hit = exp_id == pick
        sel = sel | hit
        cand = jnp.where(hit, NEG_INF, cand)
        w = jnp.where(hit, scores3, 0.0)
        weights.append(jnp.sum(jnp.sum(w, axis=1, keepdims=True), axis=0, keepdims=True))
        picks.append(pick)

    sel_f = jnp.where(sel, 1.0, 0.0).reshape(N_EXPERTS, tm)
    prefix = (_dot(sel_f.astype(BF16), tri_ref[...]) + run_ref[:, 0:1]).reshape(shape3)
    ranks = []
    for pick in picks:
        r = jnp.where(exp_id == pick, prefix, 0.0)
        ranks.append(jnp.sum(jnp.sum(r, axis=1, keepdims=True), axis=0, keepdims=True))
    new_run = run_ref[...] + jnp.sum(sel_f, axis=1, keepdims=True)
    run_ref[...] = new_run
    cnt_ref[...] = new_run.astype(I32)

    total = weights[0]
    for w in weights[1:]:
        total = total + w
    row = lax.broadcasted_iota(I32, (TOP_K, tm), 0)

    def stack(vals):
        out = jnp.zeros((TOP_K, tm), vals[0].dtype)
        for k, v in enumerate(vals):
            out = jnp.where(row == k, v.reshape(1, tm), out)
        return out

    idx_ref[...] = stack(picks)
    w_ref[...] = stack([w / total * ROUTED_SCALE for w in weights])
    rank_ref[...] = stack(ranks).astype(I32)


def _route(logits_t, bias):
    t = logits_t.shape[1]
    tm = ROUTE_TILE
    tok = pl.BlockSpec((TOP_K, tm), lambda i: (0, i))
    idx, w, rank, cnt = pl.pallas_call(
        _route_kernel,
        grid=(t // tm,),
        in_specs=[pl.BlockSpec((N_EXPERTS, tm), lambda i: (0, i)),
                  pl.BlockSpec((N_EXPERTS, 1), lambda i: (0, 0))],
        out_specs=[tok, tok, tok, pl.BlockSpec((N_EXPERTS, LANES), lambda i: (0, 0))],
        out_shape=[jax.ShapeDtypeStruct((TOP_K, t), I32),
                   jax.ShapeDtypeStruct((TOP_K, t), F32),
                   jax.ShapeDtypeStruct((TOP_K, t), I32),
                   jax.ShapeDtypeStruct((N_EXPERTS, LANES), I32)],
        scratch_shapes=[pltpu.VMEM((N_EXPERTS, LANES), F32), pltpu.VMEM((tm, tm), BF16)],
        compiler_params=_params(("arbitrary",)),
        name="moe_route",
    )(logits_t, bias.reshape(N_EXPERTS, 1))
    return idx, w, rank, cnt[:, 0]


DISPATCH_TILE = 512


def _token_rows(i):
    return pl.ds(pl.multiple_of(i * TOKEN_TILE_ROWS, TOKEN_TILE_ROWS), TOKEN_TILE_ROWS)


def _block_rows(b):
    n = SLOT_BLOCK * TOKEN_TILE_ROWS
    return pl.ds(pl.multiple_of(b * n, n), n)


def _dispatch_kernel(cnt_ref, start_ref, nu_ref, pos_ref, h3_ref, x3_ref, zero_ref, zero_tile,
                     sem, pad_sem, *, n_blocks):
    i = pl.program_id(0)
    tm = DISPATCH_TILE

    def issue(t, carry):
        for k in range(TOP_K):
            pltpu.make_async_copy(h3_ref.at[_token_rows(t)],
                                  x3_ref.at[_token_rows(pos_ref[k, t])], sem
                                  ).start(priority=k % N_DMA_PRIORITIES)
        return carry

    lax.fori_loop(0, tm, issue, 0)

    @pl.when(i == 0)
    def _():
        zero_ref[...] = jnp.zeros_like(zero_ref)
        zero_tile[...] = jnp.zeros_like(zero_tile)

        def per_expert(e, n_pad):
            cnt = cnt_ref[e]
            padded = (cnt + SLOT_BLOCK - 1) // SLOT_BLOCK * SLOT_BLOCK

            def pad_row(r, c):
                pltpu.make_async_copy(zero_tile, x3_ref.at[_token_rows(start_ref[e] + r)],
                                      pad_sem).start()
                return c

            lax.fori_loop(cnt, padded, pad_row, 0)
            return n_pad + (padded - cnt)

        n_pad = lax.fori_loop(0, N_EXPERTS, per_expert, 0)

        def tail_block(b, c):
            pltpu.make_async_copy(zero_ref, x3_ref.at[_block_rows(b)], pad_sem).start()
            return c

        lax.fori_loop(nu_ref[0], n_blocks, tail_block, 0)

        def pad_wait(r, c):
            pltpu.make_async_copy(zero_tile, x3_ref.at[_token_rows(0)], pad_sem).wait()
            return c

        lax.fori_loop(0, n_pad, pad_wait, 0)

        def tail_wait(b, c):
            pltpu.make_async_copy(zero_ref, x3_ref.at[_block_rows(0)], pad_sem).wait()
            return c

        lax.fori_loop(nu_ref[0], n_blocks, tail_wait, 0)

    n_rows = tm * TOP_K * TOKEN_TILE_ROWS
    pltpu.make_async_copy(x3_ref.at[pl.ds(0, n_rows)], x3_ref.at[pl.ds(0, n_rows)], sem).wait()


def _dispatch(h3, pos, counts, starts, n_used, n_blocks):
    t = pos.shape[1]
    tm = DISPATCH_TILE
    return pl.pallas_call(
        functools.partial(_dispatch_kernel, n_blocks=n_blocks),
        grid_spec=pltpu.PrefetchScalarGridSpec(
            num_scalar_prefetch=3,
            grid=(t // tm,),
            in_specs=[pl.BlockSpec((TOP_K, tm), lambda i, c, s, u: (0, i), memory_space=pltpu.SMEM),
                      pl.BlockSpec((tm * TOKEN_TILE_ROWS, LANES), lambda i, c, s, u: (i, 0))],
            out_specs=pl.BlockSpec(memory_space=pl.ANY),
            scratch_shapes=[pltpu.VMEM((SLOT_BLOCK * TOKEN_TILE_ROWS, LANES), I32),
                            pltpu.VMEM((TOKEN_TILE_ROWS, LANES), I32),
                            pltpu.SemaphoreType.DMA, pltpu.SemaphoreType.DMA],
        ),
        out_shape=jax.ShapeDtypeStruct((n_blocks * SLOT_BLOCK * TOKEN_TILE_ROWS, LANES), I32),
        compiler_params=_params(("arbitrary",)),
        name="moe_dispatch",
    )(counts, starts, n_used, pos, h3)


def _expert_kernel(be_ref, nu_ref, x3_ref, *refs):
    w_refs, y3_ref = refs[:-1], refs[-1]
    first = pl.program_id(0) * EXPERT_STEP_BLOCKS

    @pl.when(first >= nu_ref[0])
    def _():
        y3_ref[...] = jnp.zeros_like(y3_ref)

    @pl.when(first < nu_ref[0])
    def _():
        m = SLOT_BLOCK
        offsets = [g * m * TOKEN_TILE_ROWS for g in range(EXPERT_STEP_BLOCKS)]
        weights = [w_refs[3 * g:3 * g + 3] for g in range(EXPERT_STEP_BLOCKS)]
        xs = []
        for off in offsets:
            pieces = []
            for c in range(TOKEN_TILE_ROWS):
                lo, hi = _unpack_pair(x3_ref[pl.ds(off + c, m, stride=TOKEN_TILE_ROWS), :])
                pieces += [lo.astype(BF16), hi.astype(BF16)]
            xs.append(jnp.concatenate(pieces, axis=1))
        acts = [(_silu(_dot(x, wg[0, 0])) * _dot(x, wu[0, 0])).astype(BF16)
                for x, (wg, wu, _) in zip(xs, weights)]
        ys = [_dot(a, wd[0, 0]) for a, (_, _, wd) in zip(acts, weights)]
        for off, y in zip(offsets, ys):
            for c in range(TOKEN_TILE_ROWS):
                base = 2 * LANES * c
                y3_ref[pl.ds(off + c, m, stride=TOKEN_TILE_ROWS), :] = _pack_pair(
                    y[:, base:base + LANES], y[:, base + LANES:base + 2 * LANES])


def _experts(x3, block_expert, n_used, layer, w_gate, w_up, w_down):
    n_blocks = block_expert.shape[0]
    assert n_blocks % EXPERT_STEP_BLOCKS == 0
    rows = EXPERT_STEP_BLOCKS * SLOT_BLOCK * TOKEN_TILE_ROWS

    def expert_block(g):
        def index(s, be, nu):
            return (layer, be[jnp.minimum(s * EXPERT_STEP_BLOCKS + g, nu[0] - 1)], 0, 0)
        return index

    w_specs, w_args = [], []
    for g in range(EXPERT_STEP_BLOCKS):
        w_specs += [pl.BlockSpec((1, 1, D_MODEL, EXPERT_DIM), expert_block(g)),
                    pl.BlockSpec((1, 1, D_MODEL, EXPERT_DIM), expert_block(g)),
                    pl.BlockSpec((1, 1, EXPERT_DIM, D_MODEL), expert_block(g))]
        w_args += [w_gate, w_up, w_down]
    return pl.pallas_call(
        _expert_kernel,
        grid_spec=pltpu.PrefetchScalarGridSpec(
            num_scalar_prefetch=2,
            grid=(n_blocks // EXPERT_STEP_BLOCKS,),
            in_specs=[pl.BlockSpec((rows, LANES), lambda s, be, nu: (s, 0))] + w_specs,
            out_specs=pl.BlockSpec((rows, LANES), lambda s, be, nu: (s, 0)),
        ),
        out_shape=jax.ShapeDtypeStruct(x3.shape, I32),
        compiler_params=_params(("arbitrary",)),
        name="moe_experts",
    )(block_expert, n_used, x3, *w_args)


COMBINE_TILE = 256
COMBINE_SLOTS = 2


def _combine_kernel(pos_ref, nxt_ref, x_ref, h_ref, w_ref, gate_ref, sg_ref, su_ref, sd_ref, y3_ref,
                    o_ref, buf_ref, sems):
    j = pl.program_id(0)
    tm = COMBINE_TILE
    slot_rows = TOP_K * tm * TOKEN_TILE_ROWS

    def gather(p_ref, tile, slot):
        def issue(t, carry):
            for k in range(TOP_K):
                pltpu.make_async_copy(
                    y3_ref.at[_token_rows(p_ref[k, tile * tm + t])],
                    buf_ref.at[_token_rows((slot * TOP_K + k) * tm + t)], sems.at[slot]
                ).start(priority=k % N_DMA_PRIORITIES)
            return carry

        lax.fori_loop(0, tm, issue, 0)

    def finish(tile, slot):
        rows = slice(tile * tm, (tile + 1) * tm)
        h = h_ref[rows, :]
        a = (_silu(_dot(h, sg_ref[...])) * _dot(h, su_ref[...])).astype(BF16)
        shared = _dot(a, sd_ref[...])
        slot_buf = buf_ref.at[pl.ds(slot * slot_rows, slot_rows)]
        pltpu.make_async_copy(y3_ref.at[pl.ds(0, slot_rows)], slot_buf, sems.at[slot]).wait()
        gate = gate_ref[0]
        wk = [w_ref[rows, k:k + 1] for k in range(TOP_K)]
        for c in range(TOKEN_TILE_ROWS):
            lo_acc = None
            for k in range(TOP_K):
                word = slot_buf[pl.ds(k * tm * TOKEN_TILE_ROWS + c, tm, stride=TOKEN_TILE_ROWS), :]
                lo, hi = _unpack_pair(word)
                if lo_acc is None:
                    lo_acc, hi_acc = wk[k] * lo, wk[k] * hi
                else:
                    lo_acc, hi_acc = lo_acc + wk[k] * lo, hi_acc + wk[k] * hi
            base = 2 * LANES * c
            for off, acc in ((base, lo_acc), (base + LANES, hi_acc)):
                cols = slice(off, off + LANES)
                o_ref[rows, cols] = x_ref[rows, cols] + gate[:, cols] * (acc + shared[:, cols])

    @pl.when(j == 0)
    def _():
        gather(pos_ref, 0, 0)

    gather(pos_ref, 1, 1)
    finish(0, 0)

    @pl.when(j + 1 < pl.num_programs(0))
    def _():
        gather(nxt_ref, 0, 0)

    finish(1, 1)


def _combine(x, h, y3, pos, w_tok, mod3, gate_slot, seq_len, s_gate, s_up, s_down):
    t = x.shape[0]
    rows = COMBINE_SLOTS * COMBINE_TILE
    n_steps = t // rows
    sps = seq_len // rows
    weight = lambda shape: pl.BlockSpec(shape, lambda i: tuple(0 for _ in shape),
                                        pipeline_mode=pl.Buffered(1))
    return pl.pallas_call(
        _combine_kernel,
        grid=(n_steps,),
        in_specs=[
            pl.BlockSpec((TOP_K, rows), lambda i: (0, i), memory_space=pltpu.SMEM),
            pl.BlockSpec((TOP_K, rows), lambda i: (0, jnp.minimum(i + 1, n_steps - 1)),
                         memory_space=pltpu.SMEM),
            pl.BlockSpec((rows, D_MODEL), lambda i: (i, 0)),
            pl.BlockSpec((rows, D_MODEL), lambda i: (i, 0)),
            pl.BlockSpec((rows, TOP_K), lambda i: (i, 0)),
            _mod_spec(gate_slot, sps),
            weight((D_MODEL, EXPERT_DIM)), weight((D_MODEL, EXPERT_DIM)),
            weight((EXPERT_DIM, D_MODEL)),
            pl.BlockSpec(memory_space=pl.ANY),
        ],
        out_specs=pl.BlockSpec((rows, D_MODEL), lambda i: (i, 0)),
        out_shape=jax.ShapeDtypeStruct((t, D_MODEL), F32),
        scratch_shapes=[
            pltpu.VMEM((COMBINE_SLOTS * TOP_K * COMBINE_TILE * TOKEN_TILE_ROWS, LANES), I32),
            pltpu.SemaphoreType.DMA((COMBINE_SLOTS,))],
        compiler_params=_params(("arbitrary",)),
        name="moe_combine",
    )(pos, pos, x, h, w_tok, mod3, s_gate, s_up, s_down, y3)


def _moe_layer(x, mod3, norm_g, seq_len, layer, router_wt, router_bias, w_gate, w_up, w_down,
               s_gate, s_up, s_down):
    t = x.shape[0]
    h, h3, logits_t = _norm_mod(x, norm_g, mod3, 3, 4, seq_len, router_wt)
    idx, w, rank, counts = _route(logits_t, router_bias)
    n_blocks = -(-(t * TOP_K + N_EXPERTS * (SLOT_BLOCK - 1)) // SLOT_BLOCK)
    padded = (counts + SLOT_BLOCK - 1) // SLOT_BLOCK * SLOT_BLOCK
    padded_end = jnp.cumsum(padded)
    starts = (padded_end - padded).astype(I32)
    block_start = jnp.arange(n_blocks, dtype=I32) * SLOT_BLOCK
    block_expert = jnp.minimum(
        jnp.sum((padded_end[None, :] <= block_start[:, None]).astype(I32), axis=1), N_EXPERTS - 1)
    n_used = (padded_end[-1:] // SLOT_BLOCK).astype(I32)
    experts = jnp.arange(N_EXPERTS, dtype=I32)
    pos = jnp.sum(jnp.where(idx[:, :, None] == experts, starts, 0), axis=-1) + rank
    x3 = _dispatch(h3, pos, counts.astype(I32), starts, n_used, n_blocks)
    y3 = _experts(x3, block_expert, n_used, layer, w_gate, w_up, w_down)
    return _combine(x, h, y3, pos, w.T, mod3, 5, seq_len, s_gate, s_up, s_down)


def _trunk(x, mod, batch, seq_len, p):
    t = batch * seq_len
    x = x.reshape(t, D_MODEL)
    for layer in range(DEPTH):
        mod3 = mod[layer].reshape(batch * 6, 1, D_MODEL)
        g_mix = p["norm_mix"][layer].reshape(1, D_MODEL)
        g_ffn = p["norm_ffn"][layer].reshape(1, D_MODEL)
        h = _norm_mod(x, g_mix, mod3, 0, 1, seq_len)
        if layer % 2 == 0:
            a = layer // 2
            qkv = _matmul(h, p["attn_w_in"][a])
            o = _attention(qkv, p["attn_q_norm"][a], p["attn_k_norm"][a], p["attn_sink"][a],
                           batch, seq_len)
            x = _matmul_residual(o, p["attn_w_out"][a], x, mod3, 2, seq_len)
        else:
            g = layer // 2
            proj = _matmul(h, p["gla_w_main"][g])
            la_f, la_b = _gla_gates(h, p["gla_w_lr"][g], p["gla_w_g"][g], p["gla_b_g"][g])
            o = _gla(proj, la_f, la_b, p["gla_norm"][g], batch, seq_len)
            x = _matmul_residual(o, p["gla_w_out"][g], x, mod3, 2, seq_len)
        x = _moe_layer(x, mod3, g_ffn, seq_len, layer, p["router_wt"][layer], p["router_bias"][layer],
                       p["exp_w_gate"], p["exp_w_up"], p["exp_w_down"],
                       p["sh_w_gate"][layer], p["sh_w_up"][layer], p["sh_w_down"][layer])
    return x.reshape(batch, seq_len, D_MODEL)


def kernel(x_prompt, x_sample, c_prompt, c_sample, ada_w, ada_b, norm_mix, norm_ffn, attn_w_in, attn_q_norm, attn_k_norm, attn_sink, attn_w_out, gla_w_in, gla_w_g2_f, gla_b_g_f, gla_w_g2_b, gla_b_g_b, gla_norm, gla_w_out, router_w, router_bias, exp_w_gate, exp_w_up, exp_w_down, sh_w_gate, sh_w_up, sh_w_down):
    bp, sp, _ = x_prompt.shape
    bs, ss, _ = x_sample.shape
    n_gate = GLA_HEADS * GLA_DK
    n_layers_gla = gla_w_in.shape[0]

    w_lr = jnp.zeros((n_layers_gla, D_MODEL, GLA_GATE_PAD), F32)
    w_lr = w_lr.at[:, :, :2 * GLA_RANK].set(gla_w_in[:, :, GLA_MAIN:])
    w_g = jnp.zeros((n_layers_gla, GLA_GATE_PAD, 2 * n_gate), F32)
    w_g = w_g.at[:, :GLA_RANK, :n_gate].set(gla_w_g2_f)
    w_g = w_g.at[:, GLA_RANK:2 * GLA_RANK, n_gate:].set(gla_w_g2_b)
    p = {
        "norm_mix": norm_mix, "norm_ffn": norm_ffn,
        "attn_w_in": attn_w_in.astype(BF16), "attn_q_norm": attn_q_norm,
        "attn_k_norm": attn_k_norm, "attn_sink": attn_sink, "attn_w_out": attn_w_out.astype(BF16),
        "gla_w_main": gla_w_in[:, :, :GLA_MAIN].astype(BF16), "gla_w_lr": w_lr, "gla_w_g": w_g,
        "gla_b_g": jnp.concatenate([gla_b_g_f, gla_b_g_b], axis=-1).reshape(n_layers_gla, 1, 2 * n_gate),
        "gla_norm": gla_norm, "gla_w_out": gla_w_out.astype(BF16),
        "router_wt": jnp.swapaxes(router_w, 1, 2), "router_bias": router_bias,
        "exp_w_gate": exp_w_gate.astype(BF16), "exp_w_up": exp_w_up.astype(BF16),
        "exp_w_down": exp_w_down.astype(BF16),
        "sh_w_gate": sh_w_gate.astype(BF16), "sh_w_up": sh_w_up.astype(BF16),
        "sh_w_down": sh_w_down.astype(BF16),
    }

    c_all = jnp.zeros((SUBLANES, D_MODEL), F32).at[:bp + bs].set(jnp.concatenate([c_prompt, c_sample], 0))
    mod = _modulation(c_all, ada_w, ada_b)
    y_prompt = _trunk(x_prompt, mod[:, :bp], bp, sp, p)
    y_sample = _trunk(x_sample, mod[:, bp:bp + bs], bs, ss, p)
    return (y_prompt, y_sample)
```

```python
import functools

import numpy as np
import jax
import jax.numpy as jnp
from jax import lax
from jax.experimental import pallas as pl
from jax.experimental.pallas import tpu as pltpu

F32 = jnp.float32
BF16 = jnp.bfloat16
I32 = jnp.int32

D_MODEL = 2048
DEPTH = 2
EPS = 1e-6
HEAD_DIM = 128
N_Q_HEADS = 16
N_KV_HEADS = 4
Q_PER_KV = 4
WINDOW = 128
ATTN_BLOCK = 128
GLA_HEADS = 4
GLA_DK = 256
GLA_DV = 512
GLA_RANK = 16
GLA_TAU = 16.0
GLA_CHUNK = 64
GLA_MAIN = 2 * GLA_HEADS * GLA_DK + 2 * GLA_HEADS * GLA_DV
N_EXPERTS = 64
TOP_K = 8
N_GROUPS = 8
GROUP_SIZE = N_EXPERTS // N_GROUPS
TOPK_GROUPS = 4
EXPERT_DIM = 512
ROUTED_SCALE = 2.5
SLOT_BLOCK = 256
EXPERT_STEP_BLOCKS = 2

SUBLANES = 8
LANES = 128
TOKEN_TILE_ROWS = D_MODEL // (2 * LANES)
VMEM_LIMIT_BYTES = 56 * 1024 * 1024
N_DMA_PRIORITIES = 2

NEG_INF = float("-inf")


def _params(semantics):
    return pltpu.CompilerParams(dimension_semantics=semantics, vmem_limit_bytes=VMEM_LIMIT_BYTES)


def _dot(a, b):
    return jnp.dot(a, b, preferred_element_type=F32)


def _dot_nt(a, b):
    return lax.dot_general(a, b, (((1,), (1,)), ((), ())), preferred_element_type=F32)


def _dot_tn(a, b):
    return lax.dot_general(a, b, (((0,), (0,)), ((), ())), preferred_element_type=F32)


def _split3(x):
    hi = x.astype(BF16)
    r1 = x - hi.astype(F32)
    mid = r1.astype(BF16)
    lo = (r1 - mid.astype(F32)).astype(BF16)
    return hi, mid, lo


def _sigmoid(x):
    return 1.0 / (1.0 + jnp.exp(-x))


def _silu(x):
    return x * _sigmoid(x)


def _pack_pair(lo, hi):
    lo_bits = lax.bitcast_convert_type(lo.astype(BF16).astype(F32), I32)
    hi_bits = lax.bitcast_convert_type(hi.astype(BF16).astype(F32), I32)
    return lax.shift_right_logical(lo_bits, 16) | (hi_bits & jnp.int32(-65536))


def _unpack_pair(word):
    lo = lax.bitcast_convert_type(lax.shift_left(word, 16), F32)
    hi = lax.bitcast_convert_type(word & jnp.int32(-65536), F32)
    return lo, hi


def _mod_kernel(c_ref, w_ref, b_ref, o_ref):
    a = _silu(c_ref[...])
    o_ref[0] = jnp.dot(a, w_ref[0], preferred_element_type=F32,
                       precision=lax.Precision.HIGHEST) + b_ref[0]


def _modulation(c_all, ada_w, ada_b):
    n = 6 * D_MODEL
    tn = 1024
    return pl.pallas_call(
        _mod_kernel,
        grid=(DEPTH, n // tn),
        in_specs=[
            pl.BlockSpec((SUBLANES, D_MODEL), lambda l, j: (0, 0)),
            pl.BlockSpec((1, D_MODEL, tn), lambda l, j: (l, 0, j)),
            pl.BlockSpec((1, 1, tn), lambda l, j: (l, 0, j)),
        ],
        out_specs=pl.BlockSpec((1, SUBLANES, tn), lambda l, j: (l, 0, j)),
        out_shape=jax.ShapeDtypeStruct((DEPTH, SUBLANES, n), F32),
        compiler_params=_params(("arbitrary", "arbitrary")),
        name="adaln_mod",
    )(c_all, ada_w, ada_b.reshape(DEPTH, 1, n))


def _normed(x_ref, g_ref, sc_ref, sh_ref):
    x = x_ref[...]
    ms = jnp.mean(x * x, axis=-1, keepdims=True)
    y = x * lax.rsqrt(ms + EPS) * g_ref[...]
    return y * (1.0 + sc_ref[0]) + sh_ref[0]


def _norm_kernel(x_ref, g_ref, sc_ref, sh_ref, h_ref):
    h_ref[...] = _normed(x_ref, g_ref, sc_ref, sh_ref).astype(BF16)


def _norm_router_kernel(x_ref, g_ref, sc_ref, sh_ref, wr_ref, h_ref, h3_ref, lg_ref):
    h = _normed(x_ref, g_ref, sc_ref, sh_ref)
    tm = h.shape[0]
    h_hi = h.astype(BF16)
    h_ref[...] = h_hi
    h_lo = (h - h_hi.astype(F32)).astype(BF16)
    w_hi, w_mid, _ = _split3(wr_ref[...])
    lg_ref[...] = _dot_nt(w_hi, h_hi) + _dot_nt(w_hi, h_lo) + _dot_nt(w_mid, h_hi)
    h_r = h_hi.astype(F32)
    for c in range(TOKEN_TILE_ROWS):
        base = 2 * LANES * c
        word = _pack_pair(h_r[:, base:base + LANES], h_r[:, base + LANES:base + 2 * LANES])
        h3_ref[pl.ds(c, tm, stride=TOKEN_TILE_ROWS), :] = word


def _mod_spec(slot, steps_per_seq, width=D_MODEL):
    return pl.BlockSpec((1, 1, width), lambda i: (6 * (i // steps_per_seq) + slot, 0, 0))


def _norm_mod(x, g, mod3, shift_slot, scale_slot, seq_len, router_wt=None):
    t = x.shape[0]
    tm = 512
    sps = seq_len // tm
    in_specs = [
        pl.BlockSpec((tm, D_MODEL), lambda i: (i, 0)),
        pl.BlockSpec((1, D_MODEL), lambda i: (0, 0)),
        _mod_spec(scale_slot, sps),
        _mod_spec(shift_slot, sps),
    ]
    h_spec = pl.BlockSpec((tm, D_MODEL), lambda i: (i, 0))
    h_shape = jax.ShapeDtypeStruct((t, D_MODEL), BF16)
    if router_wt is None:
        return pl.pallas_call(
            _norm_kernel, grid=(t // tm,), in_specs=in_specs, out_specs=h_spec, out_shape=h_shape,
            compiler_params=_params(("arbitrary",)), name="norm_mod",
        )(x, g, mod3, mod3)
    return pl.pallas_call(
        _norm_router_kernel,
        grid=(t // tm,),
        in_specs=in_specs + [pl.BlockSpec((N_EXPERTS, D_MODEL), lambda i: (0, 0))],
        out_specs=[
            h_spec,
            pl.BlockSpec((tm * TOKEN_TILE_ROWS, LANES), lambda i: (i, 0)),
            pl.BlockSpec((N_EXPERTS, tm), lambda i: (0, i)),
        ],
        out_shape=[
            h_shape,
            jax.ShapeDtypeStruct((t * TOKEN_TILE_ROWS, LANES), I32),
            jax.ShapeDtypeStruct((N_EXPERTS, t), F32),
        ],
        compiler_params=_params(("arbitrary",)),
        name="norm_mod_router",
    )(x, g, mod3, mod3, router_wt)


def _mm_kernel(a_ref, w_ref, o_ref):
    o_ref[...] = _dot(a_ref[...], w_ref[...]).astype(o_ref.dtype)


def _mm_res_kernel(a_ref, w_ref, x_ref, gate_ref, o_ref):
    o_ref[...] = x_ref[...] + gate_ref[0] * _dot(a_ref[...], w_ref[...])


def _matmul(a, w, out_dtype=BF16, tm=1024, tn=512):
    m, k = a.shape
    n = w.shape[1]
    return pl.pallas_call(
        _mm_kernel,
        grid=(m // tm, n // tn),
        in_specs=[pl.BlockSpec((tm, k), lambda i, j: (i, 0)),
                  pl.BlockSpec((k, tn), lambda i, j: (0, j))],
        out_specs=pl.BlockSpec((tm, tn), lambda i, j: (i, j)),
        out_shape=jax.ShapeDtypeStruct((m, n), out_dtype),
        compiler_params=_params(("arbitrary", "arbitrary")),
        name="matmul",
    )(a, w)


def _matmul_residual(a, w, x, mod3, gate_slot, seq_len, tm=1024, tn=512):
    m, k = a.shape
    n = w.shape[1]
    sps = seq_len // tm
    return pl.pallas_call(
        _mm_res_kernel,
        grid=(m // tm, n // tn),
        in_specs=[pl.BlockSpec((tm, k), lambda i, j: (i, 0)),
                  pl.BlockSpec((k, tn), lambda i, j: (0, j)),
                  pl.BlockSpec((tm, tn), lambda i, j: (i, j)),
                  pl.BlockSpec((1, 1, tn), lambda i, j: (6 * (i // sps) + gate_slot, 0, j))],
        out_specs=pl.BlockSpec((tm, tn), lambda i, j: (i, j)),
        out_shape=jax.ShapeDtypeStruct((m, n), F32),
        compiler_params=_params(("arbitrary", "arbitrary")),
        name="matmul_residual",
    )(a, w, x, mod3)


_ALIBI_SLOPES = [float(s) for s in
                 np.exp2(-8.0 * np.arange(1, N_Q_HEADS + 1, dtype=np.float32) / N_Q_HEADS)]


def _head_norm(x, gain):
    xf = x.astype(F32)
    ms = jnp.mean(xf * xf, axis=-1, keepdims=True)
    return xf * lax.rsqrt(ms + EPS) * gain


def _attn_kernel(sink_ref, q_ref, kp_ref, kc_ref, kn_ref, vp_ref, vc_ref, vn_ref,
                 qg_ref, kg_ref, o_ref, *, n_blocks):
    i = pl.program_id(1)
    span = ATTN_BLOCK + 2 * WINDOW
    t_idx = lax.broadcasted_iota(I32, (ATTN_BLOCK, span), 0)
    j_idx = lax.broadcasted_iota(I32, (ATTN_BLOCK, span), 1)
    dist = jnp.abs(t_idx - j_idx + WINDOW)
    valid = dist <= WINDOW
    valid = valid & ((j_idx >= WINDOW) | (i > 0))
    valid = valid & ((j_idx < WINDOW + ATTN_BLOCK) | (i < n_blocks - 1))
    dist_f = dist.astype(F32)
    q_gain = qg_ref[...] * (HEAD_DIM ** -0.5)
    k_gain = kg_ref[...]
    for g in range(N_KV_HEADS):
        cols = slice(g * HEAD_DIM, (g + 1) * HEAD_DIM)
        k_all = jnp.concatenate([kp_ref[:, cols], kc_ref[:, cols], kn_ref[:, cols]], axis=0)
        v_all = jnp.concatenate([vp_ref[:, cols], vc_ref[:, cols], vn_ref[:, cols]], axis=0)
        k_n = _head_norm(k_all, k_gain).astype(BF16)
        q_n = []
        for jh in range(Q_PER_KV):
            h = g * Q_PER_KV + jh
            q_n.append(_head_norm(q_ref[:, h * HEAD_DIM:(h + 1) * HEAD_DIM], q_gain).astype(BF16))
        s_all = _dot_nt(jnp.concatenate(q_n, axis=0), k_n)
        p_all, inv_den = [], []
        for jh in range(Q_PER_KV):
            h = g * Q_PER_KV + jh
            sink = sink_ref[h]
            s = s_all[jh * ATTN_BLOCK:(jh + 1) * ATTN_BLOCK] - _ALIBI_SLOPES[h] * dist_f
            s = jnp.where(valid, s, NEG_INF)
            m = jnp.maximum(jnp.max(s, axis=-1, keepdims=True), sink)
            p = jnp.exp(s - m)
            den = jnp.sum(p, axis=-1, keepdims=True) + jnp.exp(sink - m)
            p_all.append(p.astype(BF16))
            inv_den.append(1.0 / den)
        o_all = _dot(jnp.concatenate(p_all, axis=0), v_all)
        outs = [o_all[jh * ATTN_BLOCK:(jh + 1) * ATTN_BLOCK] * inv_den[jh] for jh in range(Q_PER_KV)]
        o_ref[:, g * Q_PER_KV * HEAD_DIM:(g + 1) * Q_PER_KV * HEAD_DIM] = (
            jnp.concatenate(outs, axis=1).astype(BF16))


def _attention(qkv, q_gain, k_gain, sink, batch, seq_len):
    t = qkv.shape[0]
    nb = seq_len // ATTN_BLOCK
    qw = N_Q_HEADS * HEAD_DIM
    kvw = N_KV_HEADS * HEAD_DIM
    k_col = qw // kvw
    v_col = k_col + 1

    def rows(b, i):
        return b * nb + i

    def kv_spec(col, shift):
        return pl.BlockSpec(
            (ATTN_BLOCK, kvw),
            lambda b, i: (rows(b, jnp.clip(i + shift, 0, nb - 1)), col))

    return pl.pallas_call(
        functools.partial(_attn_kernel, n_blocks=nb),
        grid=(batch, nb),
        in_specs=[
            pl.BlockSpec(memory_space=pltpu.SMEM),
            pl.BlockSpec((ATTN_BLOCK, qw), lambda b, i: (rows(b, i), 0)),
            kv_spec(k_col, -1), kv_spec(k_col, 0), kv_spec(k_col, 1),
            kv_spec(v_col, -1), kv_spec(v_col, 0), kv_spec(v_col, 1),
            pl.BlockSpec((1, HEAD_DIM), lambda b, i: (0, 0)),
            pl.BlockSpec((1, HEAD_DIM), lambda b, i: (0, 0)),
        ],
        out_specs=pl.BlockSpec((ATTN_BLOCK, qw), lambda b, i: (rows(b, i), 0)),
        out_shape=jax.ShapeDtypeStruct((t, qw), BF16),
        compiler_params=_params(("arbitrary", "arbitrary")),
        name="window_attention",
    )(sink, qkv, qkv, qkv, qkv, qkv, qkv, qkv, q_gain.reshape(1, HEAD_DIM), k_gain.reshape(1, HEAD_DIM))


GLA_GATE_PAD = LANES


def _gla_gate_kernel(h_ref, wlr_ref, wg_ref, bg_ref, laf_ref, lab_ref):
    h = h_ref[...]
    w_hi, w_mid, _ = _split3(wlr_ref[...])
    lr = _dot(h, w_hi) + _dot(h, w_mid)
    l_hi, l_mid, l_lo = _split3(lr)
    g_hi, g_mid, _ = _split3(wg_ref[...])
    z = (_dot(l_hi, g_hi) + _dot(l_mid, g_hi) + _dot(l_hi, g_mid) + _dot(l_lo, g_hi)
         + bg_ref[...])
    log_sig = jnp.minimum(z, 0.0) - jnp.log(1.0 + jnp.exp(-jnp.abs(z)))
    la = log_sig / GLA_TAU
    n = GLA_HEADS * GLA_DK
    laf_ref[...] = la[:, :n]
    lab_ref[...] = la[:, n:]


def _gla_gates(h, w_lr, w_g, b_g):
    t = h.shape[0]
    tm = 512
    n = GLA_HEADS * GLA_DK
    out = jax.ShapeDtypeStruct((t, n), F32)
    return pl.pallas_call(
        _gla_gate_kernel,
        grid=(t // tm,),
        in_specs=[pl.BlockSpec((tm, D_MODEL), lambda i: (i, 0)),
                  pl.BlockSpec((D_MODEL, GLA_GATE_PAD), lambda i: (0, 0)),
                  pl.BlockSpec((GLA_GATE_PAD, 2 * n), lambda i: (0, 0)),
                  pl.BlockSpec((1, 2 * n), lambda i: (0, 0))],
        out_specs=[pl.BlockSpec((tm, n), lambda i: (i, 0)), pl.BlockSpec((tm, n), lambda i: (i, 0))],
        out_shape=[out, out],
        compiler_params=_params(("arbitrary",)),
        name="gla_gates",
    )(h, w_lr, w_g, b_g)


GLA_STEP = 256


def _gla_chunks(q_ref, k_ref, v_ref, la_ref, st_ref, reverse):
    shift = GLA_CHUNK.bit_length() - 1
    r_idx = lax.broadcasted_iota(I32, (GLA_STEP, GLA_STEP), 0)
    c_idx = lax.broadcasted_iota(I32, (GLA_STEP, GLA_STEP), 1)
    same_chunk = lax.shift_right_logical(r_idx, shift) == lax.shift_right_logical(c_idx, shift)
    if reverse:
        sum_mat = same_chunk & (c_idx >= r_idx)
        keep = same_chunk & (c_idx > r_idx)
    else:
        sum_mat = same_chunk & (c_idx <= r_idx)
        keep = sum_mat
    sum_mat = jnp.where(sum_mat, 1.0, 0.0).astype(BF16)
    chunk_mat = jnp.where(same_chunk, 1.0, 0.0).astype(BF16)
    n_chunks = GLA_STEP // GLA_CHUNK
    la_hi, la_mid, la_lo = _split3(la_ref[...])
    b = _dot(sum_mat, la_hi) + _dot(sum_mat, la_mid) + _dot(sum_mat, la_lo)
    total = _dot(chunk_mat, la_hi) + _dot(chunk_mat, la_mid) + _dot(chunk_mat, la_lo)
    q = q_ref[...].astype(F32) * (GLA_DK ** -0.5)
    k = k_ref[...].astype(F32)
    v = v_ref[...]
    q_dec = (q * jnp.exp(b)).astype(BF16)
    k_inv = (k * jnp.exp(-b)).astype(BF16)
    k_end = (k * jnp.exp(total - b)).astype(BF16)
    decay = jnp.exp(total)
    a = jnp.where(keep, _dot_nt(q_dec, k_inv), 0.0).astype(BF16)
    o_intra = _dot(a, v)
    order = range(n_chunks - 1, -1, -1) if reverse else range(n_chunks)
    outs = [None] * n_chunks
    st = st_ref[...]
    for c in order:
        lo = c * GLA_CHUNK
        rows = slice(lo, lo + GLA_CHUNK)
        outs[c] = _dot_nt(q_dec[rows], st.astype(BF16))
        st = st * decay[lo:lo + 1, :] + _dot_tn(v[rows], k_end[rows])
    st_ref[...] = st
    return o_intra + jnp.concatenate(outs, axis=0)


def _gla_fwd_kernel(q_ref, k_ref, v_ref, la_ref, o_ref, st_ref):
    @pl.when(pl.program_id(2) == 0)
    def _():
        st_ref[...] = jnp.zeros_like(st_ref)
    o_ref[...] = _gla_chunks(q_ref, k_ref, v_ref, la_ref, st_ref, reverse=False)


def _gla_bwd_kernel(q_ref, k_ref, v_ref, la_ref, of_ref, r_ref, ng_ref, o_ref, st_ref):
    @pl.when(pl.program_id(2) == 0)
    def _():
        st_ref[...] = jnp.zeros_like(st_ref)
    o = of_ref[...] + _gla_chunks(q_ref, k_ref, v_ref, la_ref, st_ref, reverse=True)
    ms = jnp.mean(o * o, axis=-1, keepdims=True)
    o = o * lax.rsqrt(ms + EPS) * ng_ref[...]
    o_ref[...] = (o * _silu(r_ref[...].astype(F32))).astype(BF16)


def _gla(proj, la_f, la_b, norm_g, batch, seq_len):
    t = proj.shape[0]
    nblk = seq_len // GLA_STEP
    k_col = GLA_HEADS
    v_col = 2 * GLA_HEADS * GLA_DK // GLA_DV
    r_col = v_col + GLA_HEADS

    def specs(row):
        return [
            pl.BlockSpec((GLA_STEP, GLA_DK), lambda b, h, n: (row(b, n), h)),
            pl.BlockSpec((GLA_STEP, GLA_DK), lambda b, h, n: (row(b, n), k_col + h)),
            pl.BlockSpec((GLA_STEP, GLA_DV), lambda b, h, n: (row(b, n), v_col + h)),
            pl.BlockSpec((GLA_STEP, GLA_DK), lambda b, h, n: (row(b, n), h)),
        ]

    def out_spec(row):
        return pl.BlockSpec((GLA_STEP, GLA_DV), lambda b, h, n: (row(b, n), h))

    def fwd_row(b, n):
        return b * nblk + n

    def bwd_row(b, n):
        return b * nblk + (nblk - 1 - n)

    grid = (batch, GLA_HEADS, nblk)
    state = pltpu.VMEM((GLA_DV, GLA_DK), F32)
    sem = ("arbitrary", "arbitrary", "arbitrary")
    o_f = pl.pallas_call(
        _gla_fwd_kernel, grid=grid, in_specs=specs(fwd_row), out_specs=out_spec(fwd_row),
        out_shape=jax.ShapeDtypeStruct((t, GLA_HEADS * GLA_DV), F32),
        scratch_shapes=[state], compiler_params=_params(sem), name="gla_forward",
    )(proj, proj, proj, la_f)
    return pl.pallas_call(
        _gla_bwd_kernel, grid=grid,
        in_specs=specs(bwd_row) + [
            out_spec(bwd_row),
            pl.BlockSpec((GLA_STEP, GLA_DV), lambda b, h, n: (bwd_row(b, n), r_col + h)),
            pl.BlockSpec((1, GLA_DV), lambda b, h, n: (0, 0)),
        ],
        out_specs=out_spec(bwd_row),
        out_shape=jax.ShapeDtypeStruct((t, GLA_HEADS * GLA_DV), BF16),
        scratch_shapes=[state], compiler_params=_params(sem), name="gla_backward",
    )(proj, proj, proj, la_b, o_f, proj, norm_g.reshape(1, GLA_DV))


ROUTE_TILE = 512


def _route_kernel(lg_ref, bias_ref, idx_ref, w_ref, rank_ref, cnt_ref, run_ref, tri_ref):
    tm = ROUTE_TILE

    @pl.when(pl.program_id(0) == 0)
    def _():
        run_ref[...] = jnp.zeros_like(run_ref)
        r = lax.broadcasted_iota(I32, (tm, tm), 0)
        c = lax.broadcasted_iota(I32, (tm, tm), 1)
        tri_ref[...] = (r < c).astype(BF16)

    shape3 = (N_GROUPS, GROUP_SIZE, tm)
    scores = _sigmoid(lg_ref[...])
    biased = (scores + bias_ref[...]).reshape(shape3)
    scores3 = scores.reshape(shape3)
    in_grp = lax.broadcasted_iota(I32, shape3, 1)
    grp_id = lax.broadcasted_iota(I32, shape3, 0)
    exp_id = grp_id * GROUP_SIZE + in_grp

    m1 = jnp.max(biased, axis=1, keepdims=True)
    first = jnp.min(jnp.where(biased == m1, in_grp, GROUP_SIZE), axis=1, keepdims=True)
    m2 = jnp.max(jnp.where(in_grp == first, NEG_INF, biased), axis=1, keepdims=True)
    grp_score = m1 + m2

    g_iota = lax.broadcasted_iota(I32, (N_GROUPS, 1, tm), 0)
    g_sel = jnp.zeros((N_GROUPS, 1, tm), dtype=jnp.bool_)
    cur = grp_score
    for _ in range(TOPK_GROUPS):
        m = jnp.max(cur, axis=0, keepdims=True)
        pick = jnp.min(jnp.where(cur == m, g_iota, N_GROUPS), axis=0, keepdims=True)
        hit = g_iota == pick
        g_sel = g_sel | hit
        cur = jnp.where(hit, NEG_INF, cur)

    cand = jnp.where(g_sel, biased, NEG_INF)
    picks, weights = [], []
    sel = jnp.zeros(shape3, dtype=jnp.bool_)
    for _ in range(TOP_K):
        m = jnp.max(jnp.max(cand, axis=1, keepdims=True), axis=0, keepdims=True)
        pick = jnp.where(cand == m, exp_id, N_EXPERTS)
        pick = jnp.min(jnp.min(pick, axis=1, keepdims=True), axis=0, keepdims=True)
        hit = exp_id == pick
        sel = sel | hit
        cand = jnp.where(hit, NEG_INF, cand)
        w = jnp.where(hit, scores3, 0.0)
        weights.append(jnp.sum(jnp.sum(w, axis=1, keepdims=True), axis=0, keepdims=True))
        picks.append(pick)

    sel_f = jnp.where(sel, 1.0, 0.0).reshape(N_EXPERTS, tm)
    prefix = (_dot(sel_f.astype(BF16), tri_ref[...]) + run_ref[:, 0:1]).reshape(shape3)
    ranks = []
    for pick in picks:
        r = jnp.where(exp_id == pick, prefix, 0.0)
        ranks.append(jnp.sum(jnp.sum(r, axis=1, keepdims=True), axis=0, keepdims=True))
    new_run = run_ref[...] + jnp.sum(sel_f, axis=1, keepdims=True)
    run_ref[...] = new_run
    cnt_ref[...] = new_run.astype(I32)

    total = weights[0]
    for w in weights[1:]:
        total = total + w
    row = lax.broadcasted_iota(I32, (TOP_K, tm), 0)

    def stack(vals):
        out = jnp.zeros((TOP_K, tm), vals[0].dtype)
        for k, v in enumerate(vals):
            out = jnp.where(row == k, v.reshape(1, tm), out)
        return out

    idx_ref[...] = stack(picks)
    w_ref[...] = stack([w / total * ROUTED_SCALE for w in weights])
    rank_ref[...] = stack(ranks).astype(I32)


def _route(logits_t, bias):
    t = logits_t.shape[1]
    tm = ROUTE_TILE
    tok = pl.BlockSpec((TOP_K, tm), lambda i: (0, i))
    idx, w, rank, cnt = pl.pallas_call(
        _route_kernel,
        grid=(t // tm,),
        in_specs=[pl.BlockSpec((N_EXPERTS, tm), lambda i: (0, i)),
                  pl.BlockSpec((N_EXPERTS, 1), lambda i: (0, 0))],
        out_specs=[tok, tok, tok, pl.BlockSpec((N_EXPERTS, LANES), lambda i: (0, 0))],
        out_shape=[jax.ShapeDtypeStruct((TOP_K, t), I32),
                   jax.ShapeDtypeStruct((TOP_K, t), F32),
                   jax.ShapeDtypeStruct((TOP_K, t), I32),
                   jax.ShapeDtypeStruct((N_EXPERTS, LANES), I32)],
        scratch_shapes=[pltpu.VMEM((N_EXPERTS, LANES), F32), pltpu.VMEM((tm, tm), BF16)],
        compiler_params=_params(("arbitrary",)),
        name="moe_route",
    )(logits_t, bias.reshape(N_EXPERTS, 1))
    return idx, w, rank, cnt[:, 0]


DISPATCH_TILE = 512


def _token_rows(i):
    return pl.ds(pl.multiple_of(i * TOKEN_TILE_ROWS, TOKEN_TILE_ROWS), TOKEN_TILE_ROWS)


def _block_rows(b):
    n = SLOT_BLOCK * TOKEN_TILE_ROWS
    return pl.ds(pl.multiple_of(b * n, n), n)


def _dispatch_kernel(cnt_ref, start_ref, nu_ref, pos_ref, h3_ref, x3_ref, zero_ref, zero_tile,
                     sem, pad_sem, *, n_blocks):
    i = pl.program_id(0)
    tm = DISPATCH_TILE

    def issue(t, carry):
        for k in range(TOP_K):
            pltpu.make_async_copy(h3_ref.at[_token_rows(t)],
                                  x3_ref.at[_token_rows(pos_ref[k, t])], sem
                                  ).start(priority=k % N_DMA_PRIORITIES)
        return carry

    lax.fori_loop(0, tm, issue, 0)

    @pl.when(i == 0)
    def _():
        zero_ref[...] = jnp.zeros_like(zero_ref)
        zero_tile[...] = jnp.zeros_like(zero_tile)

        def per_expert(e, n_pad):
            cnt = cnt_ref[e]
            padded = (cnt + SLOT_BLOCK - 1) // SLOT_BLOCK * SLOT_BLOCK

            def pad_row(r, c):
                pltpu.make_async_copy(zero_tile, x3_ref.at[_token_rows(start_ref[e] + r)],
                                      pad_sem).start()
                return c

            lax.fori_loop(cnt, padded, pad_row, 0)
            return n_pad + (padded - cnt)

        n_pad = lax.fori_loop(0, N_EXPERTS, per_expert, 0)

        def tail_block(b, c):
            pltpu.make_async_copy(zero_ref, x3_ref.at[_block_rows(b)], pad_sem).start()
            return c

        lax.fori_loop(nu_ref[0], n_blocks, tail_block, 0)

        def pad_wait(r, c):
            pltpu.make_async_copy(zero_tile, x3_ref.at[_token_rows(0)], pad_sem).wait()
            return c

        lax.fori_loop(0, n_pad, pad_wait, 0)

        def tail_wait(b, c):
            pltpu.make_async_copy(zero_ref, x3_ref.at[_block_rows(0)], pad_sem).wait()
            return c

        lax.fori_loop(nu_ref[0], n_blocks, tail_wait, 0)

    n_rows = tm * TOP_K * TOKEN_TILE_ROWS
    pltpu.make_async_copy(x3_ref.at[pl.ds(0, n_rows)], x3_ref.at[pl.ds(0, n_rows)], sem).wait()


def _dispatch(h3, pos, counts, starts, n_used, n_blocks):
    t = pos.shape[1]
    tm = DISPATCH_TILE
    return pl.pallas_call(
        functools.partial(_dispatch_kernel, n_blocks=n_blocks),
        grid_spec=pltpu.PrefetchScalarGridSpec(
            num_scalar_prefetch=3,
            grid=(t // tm,),
            in_specs=[pl.BlockSpec((TOP_K, tm), lambda i, c, s, u: (0, i), memory_space=pltpu.SMEM),
                      pl.BlockSpec((tm * TOKEN_TILE_ROWS, LANES), lambda i, c, s, u: (i, 0))],
            out_specs=pl.BlockSpec(memory_space=pl.ANY),
            scratch_shapes=[pltpu.VMEM((SLOT_BLOCK * TOKEN_TILE_ROWS, LANES), I32),
                            pltpu.VMEM((TOKEN_TILE_ROWS, LANES), I32),
                            pltpu.SemaphoreType.DMA, pltpu.SemaphoreType.DMA],
        ),
        out_shape=jax.ShapeDtypeStruct((n_blocks * SLOT_BLOCK * TOKEN_TILE_ROWS, LANES), I32),
        compiler_params=_params(("arbitrary",)),
        name="moe_dispatch",
    )(counts, starts, n_used, pos, h3)


def _expert_kernel(be_ref, nu_ref, x3_ref, *refs):
    w_refs, y3_ref = refs[:-1], refs[-1]
    first = pl.program_id(0) * EXPERT_STEP_BLOCKS

    @pl.when(first >= nu_ref[0])
    def _():
        y3_ref[...] = jnp.zeros_like(y3_ref)

    @pl.when(first < nu_ref[0])
    def _():
        m = SLOT_BLOCK
        offsets = [g * m * TOKEN_TILE_ROWS for g in range(EXPERT_STEP_BLOCKS)]
        weights = [w_refs[3 * g:3 * g + 3] for g in range(EXPERT_STEP_BLOCKS)]
        xs = []
        for off in offsets:
            pieces = []
            for c in range(TOKEN_TILE_ROWS):
                lo, hi = _unpack_pair(x3_ref[pl.ds(off + c, m, stride=TOKEN_TILE_ROWS), :])
                pieces += [lo.astype(BF16), hi.astype(BF16)]
            xs.append(jnp.concatenate(pieces, axis=1))
        acts = [(_silu(_dot(x, wg[0, 0])) * _dot(x, wu[0, 0])).astype(BF16)
                for x, (wg, wu, _) in zip(xs, weights)]
        ys = [_dot(a, wd[0, 0]) for a, (_, _, wd) in zip(acts, weights)]
        for off, y in zip(offsets, ys):
            for c in range(TOKEN_TILE_ROWS):
                base = 2 * LANES * c
                y3_ref[pl.ds(off + c, m, stride=TOKEN_TILE_ROWS), :] = _pack_pair(
                    y[:, base:base + LANES], y[:, base + LANES:base + 2 * LANES])


def _experts(x3, block_expert, n_used, layer, w_gate, w_up, w_down):
    n_blocks = block_expert.shape[0]
    assert n_blocks % EXPERT_STEP_BLOCKS == 0
    rows = EXPERT_STEP_BLOCKS * SLOT_BLOCK * TOKEN_TILE_ROWS

    def expert_block(g):
        def index(s, be, nu):
            return (layer, be[jnp.minimum(s * EXPERT_STEP_BLOCKS + g, nu[0] - 1)], 0, 0)
        return index

    w_specs, w_args = [], []
    for g in range(EXPERT_STEP_BLOCKS):
        w_specs += [pl.BlockSpec((1, 1, D_MODEL, EXPERT_DIM), expert_block(g)),
                    pl.BlockSpec((1, 1, D_MODEL, EXPERT_DIM), expert_block(g)),
                    pl.BlockSpec((1, 1, EXPERT_DIM, D_MODEL), expert_block(g))]
        w_args += [w_gate, w_up, w_down]
    return pl.pallas_call(
        _expert_kernel,
        grid_spec=pltpu.PrefetchScalarGridSpec(
            num_scalar_prefetch=2,
            grid=(n_blocks // EXPERT_STEP_BLOCKS,),
            in_specs=[pl.BlockSpec((rows, LANES), lambda s, be, nu: (s, 0))] + w_specs,
            out_specs=pl.BlockSpec((rows, LANES), lambda s, be, nu: (s, 0)),
        ),
        out_shape=jax.ShapeDtypeStruct(x3.shape, I32),
        compiler_params=_params(("arbitrary",)),
        name="moe_experts",
    )(block_expert, n_used, x3, *w_args)


COMBINE_TILE = 256


def _combine_kernel(pos_ref, x_ref, h_ref, w_ref, gate_ref, sg_ref, su_ref, sd_ref, y3_ref,
                    o_ref, buf_ref, sem):
    tm = COMBINE_TILE

    def issue(t, carry):
        for k in range(TOP_K):
            pltpu.make_async_copy(y3_ref.at[_token_rows(pos_ref[t * TOP_K + k])],
                                  buf_ref.at[_token_rows(k * tm + t)], sem
                                  ).start(priority=k % N_DMA_PRIORITIES)
        return carry

    lax.fori_loop(0, tm, issue, 0)

    h = h_ref[...]
    a = (_silu(_dot(h, sg_ref[...])) * _dot(h, su_ref[...])).astype(BF16)
    shared = _dot(a, sd_ref[...])

    pltpu.make_async_copy(y3_ref.at[pl.ds(0, buf_ref.shape[0])], buf_ref, sem).wait()

    gate = gate_ref[0]
    wk = [w_ref[:, k:k + 1] for k in range(TOP_K)]
    for c in range(TOKEN_TILE_ROWS):
        lo_acc = None
        for k in range(TOP_K):
            word = buf_ref[pl.ds(k * tm * TOKEN_TILE_ROWS + c, tm, stride=TOKEN_TILE_ROWS), :]
            lo, hi = _unpack_pair(word)
            if lo_acc is None:
                lo_acc, hi_acc = wk[k] * lo, wk[k] * hi
            else:
                lo_acc, hi_acc = lo_acc + wk[k] * lo, hi_acc + wk[k] * hi
        base = 2 * LANES * c
        for off, acc in ((base, lo_acc), (base + LANES, hi_acc)):
            cols = slice(off, off + LANES)
            o_ref[:, cols] = x_ref[:, cols] + gate[:, cols] * (acc + shared[:, cols])


def _combine(x, h, y3, pos, w_tok, mod3, gate_slot, seq_len, s_gate, s_up, s_down):
    t = x.shape[0]
    tm = COMBINE_TILE
    sps = seq_len // tm
    full = lambda shape: pl.BlockSpec(shape, lambda i: tuple(0 for _ in shape))
    return pl.pallas_call(
        _combine_kernel,
        grid=(t // tm,),
        in_specs=[
            pl.BlockSpec((tm * TOP_K,), lambda i: (i,), memory_space=pltpu.SMEM),
            pl.BlockSpec((tm, D_MODEL), lambda i: (i, 0)),
            pl.BlockSpec((tm, D_MODEL), lambda i: (i, 0)),
            pl.BlockSpec((tm, TOP_K), lambda i: (i, 0)),
            _mod_spec(gate_slot, sps),
            full((D_MODEL, EXPERT_DIM)), full((D_MODEL, EXPERT_DIM)), full((EXPERT_DIM, D_MODEL)),
            pl.BlockSpec(memory_space=pl.ANY),
        ],
        out_specs=pl.BlockSpec((tm, D_MODEL), lambda i: (i, 0)),
        out_shape=jax.ShapeDtypeStruct((t, D_MODEL), F32),
        scratch_shapes=[pltpu.VMEM((TOP_K * tm * TOKEN_TILE_ROWS, LANES), I32),
                        pltpu.SemaphoreType.DMA],
        compiler_params=_params(("arbitrary",)),
        name="moe_combine",
    )(pos, x, h, w_tok, mod3, s_gate, s_up, s_down, y3)


def _moe_layer(x, mod3, norm_g, seq_len, layer, router_wt, router_bias, w_gate, w_up, w_down,
               s_gate, s_up, s_down):
    t = x.shape[0]
    h, h3, logits_t = _norm_mod(x, norm_g, mod3, 3, 4, seq_len, router_wt)
    idx, w, rank, counts = _route(logits_t, router_bias)
    n_blocks = -(-(t * TOP_K + N_EXPERTS * (SLOT_BLOCK - 1)) // SLOT_BLOCK)
    padded = (counts + SLOT_BLOCK - 1) // SLOT_BLOCK * SLOT_BLOCK
    padded_end = jnp.cumsum(padded)
    starts = (padded_end - padded).astype(I32)
    block_start = jnp.arange(n_blocks, dtype=I32) * SLOT_BLOCK
    block_expert = jnp.minimum(
        jnp.sum((padded_end[None, :] <= block_start[:, None]).astype(I32), axis=1), N_EXPERTS - 1)
    n_used = (padded_end[-1:] // SLOT_BLOCK).astype(I32)
    experts = jnp.arange(N_EXPERTS, dtype=I32)
    pos = jnp.sum(jnp.where(idx[:, :, None] == experts, starts, 0), axis=-1) + rank
    x3 = _dispatch(h3, pos, counts.astype(I32), starts, n_used, n_blocks)
    y3 = _experts(x3, block_expert, n_used, layer, w_gate, w_up, w_down)
    return _combine(x, h, y3, pos.T.reshape(-1), w.T, mod3, 5, seq_len, s_gate, s_up, s_down)


def _trunk(x, mod, batch, seq_len, p):
    t = batch * seq_len
    x = x.reshape(t, D_MODEL)
    for layer in range(DEPTH):
        mod3 = mod[layer].reshape(batch * 6, 1, D_MODEL)
        g_mix = p["norm_mix"][layer].reshape(1, D_MODEL)
        g_ffn = p["norm_ffn"][layer].reshape(1, D_MODEL)
        h = _norm_mod(x, g_mix, mod3, 0, 1, seq_len)
        if layer % 2 == 0:
            a = layer // 2
            qkv = _matmul(h, p["attn_w_in"][a])
            o = _attention(qkv, p["attn_q_norm"][a], p["attn_k_norm"][a], p["attn_sink"][a],
                           batch, seq_len)
            x = _matmul_residual(o, p["attn_w_out"][a], x, mod3, 2, seq_len)
        else:
            g = layer // 2
            proj = _matmul(h, p["gla_w_main"][g])
            la_f, la_b = _gla_gates(h, p["gla_w_lr"][g], p["gla_w_g"][g], p["gla_b_g"][g])
            o = _gla(proj, la_f, la_b, p["gla_norm"][g], batch, seq_len)
            x = _matmul_residual(o, p["gla_w_out"][g], x, mod3, 2, seq_len)
        x = _moe_layer(x, mod3, g_ffn, seq_len, layer, p["router_wt"][layer], p["router_bias"][layer],
                       p["exp_w_gate"], p["exp_w_up"], p["exp_w_down"],
                       p["sh_w_gate"][layer], p["sh_w_up"][layer], p["sh_w_down"][layer])
    return x.reshape(batch, seq_len, D_MODEL)


def kernel(x_prompt, x_sample, c_prompt, c_sample, ada_w, ada_b, norm_mix, norm_ffn, attn_w_in, attn_q_norm, attn_k_norm, attn_sink, attn_w_out, gla_w_in, gla_w_g2_f, gla_b_g_f, gla_w_g2_b, gla_b_g_b, gla_norm, gla_w_out, router_w, router_bias, exp_w_gate, exp_w_up, exp_w_down, sh_w_gate, sh_w_up, sh_w_down):
    bp, sp, _ = x_prompt.shape
    bs, ss, _ = x_sample.shape
    n_gate = GLA_HEADS * GLA_DK
    n_layers_gla = gla_w_in.shape[0]

    w_lr = jnp.zeros((n_layers_gla, D_MODEL, GLA_GATE_PAD), F32)
    w_lr = w_lr.at[:, :, :2 * GLA_RANK].set(gla_w_in[:, :, GLA_MAIN:])
    w_g = jnp.zeros((n_layers_gla, GLA_GATE_PAD, 2 * n_gate), F32)
    w_g = w_g.at[:, :GLA_RANK, :n_gate].set(gla_w_g2_f)
    w_g = w_g.at[:, GLA_RANK:2 * GLA_RANK, n_gate:].set(gla_w_g2_b)
    p = {
        "norm_mix": norm_mix, "norm_ffn": norm_ffn,
        "attn_w_in": attn_w_in.astype(BF16), "attn_q_norm": attn_q_norm,
        "attn_k_norm": attn_k_norm, "attn_sink": attn_sink, "attn_w_out": attn_w_out.astype(BF16),
        "gla_w_main": gla_w_in[:, :, :GLA_MAIN].astype(BF16), "gla_w_lr": w_lr, "gla_w_g": w_g,
        "gla_b_g": jnp.concatenate([gla_b_g_f, gla_b_g_b], axis=-1).reshape(n_layers_gla, 1, 2 * n_gate),
        "gla_norm": gla_norm, "gla_w_out": gla_w_out.astype(BF16),
        "router_wt": jnp.swapaxes(router_w, 1, 2), "router_bias": router_bias,
        "exp_w_gate": exp_w_gate.astype(BF16), "exp_w_up": exp_w_up.astype(BF16),
        "exp_w_down": exp_w_down.astype(BF16),
        "sh_w_gate": sh_w_gate.astype(BF16), "sh_w_up": sh_w_up.astype(BF16),
        "sh_w_down": sh_w_down.astype(BF16),
    }

    c_all = jnp.zeros((SUBLANES, D_MODEL), F32).at[:bp + bs].set(jnp.concatenate([c_prompt, c_sample], 0))
    mod = _modulation(c_all, ada_w, ada_b)
    y_prompt = _trunk(x_prompt, mod[:, :bp], bp, sp, p)
    y_sample = _trunk(x_sample, mod[:, bp:bp + bs], bs, ss, p)
    return (y_prompt, y_sample)
```

```python
import functools

import numpy as np
import jax
import jax.numpy as jnp
from jax import lax
from jax.experimental import pallas as pl
from jax.experimental.pallas import tpu as pltpu

F32 = jnp.float32
BF16 = jnp.bfloat16
I32 = jnp.int32

D_MODEL = 2048
DEPTH = 2
EPS = 1e-6
HEAD_DIM = 128
N_Q_HEADS = 16
N_KV_HEADS = 4
Q_PER_KV = 4
WINDOW = 128
ATTN_BLOCK = 128
GLA_HEADS = 4
GLA_DK = 256
GLA_DV = 512
GLA_RANK = 16
GLA_TAU = 16.0
GLA_CHUNK = 64
GLA_MAIN = 2 * GLA_HEADS * GLA_DK + 2 * GLA_HEADS * GLA_DV
N_EXPERTS = 64
TOP_K = 8
N_GROUPS = 8
GROUP_SIZE = N_EXPERTS // N_GROUPS
TOPK_GROUPS = 4
EXPERT_DIM = 512
ROUTED_SCALE = 2.5
SLOT_BLOCK = 256
EXPERT_STEP_BLOCKS = 2

SUBLANES = 8
LANES = 128
TOKEN_TILE_ROWS = D_MODEL // (2 * LANES)
VMEM_LIMIT_BYTES = 56 * 1024 * 1024
N_DMA_PRIORITIES = 2

NEG_INF = float("-inf")


def _params(semantics):
    return pltpu.CompilerParams(dimension_semantics=semantics, vmem_limit_bytes=VMEM_LIMIT_BYTES)


def _dot(a, b):
    return jnp.dot(a, b, preferred_element_type=F32)


def _dot_nt(a, b):
    return lax.dot_general(a, b, (((1,), (1,)), ((), ())), preferred_element_type=F32)


def _dot_tn(a, b):
    return lax.dot_general(a, b, (((0,), (0,)), ((), ())), preferred_element_type=F32)


def _split3(x):
    hi = x.astype(BF16)
    r1 = x - hi.astype(F32)
    mid = r1.astype(BF16)
    lo = (r1 - mid.astype(F32)).astype(BF16)
    return hi, mid, lo


def _sigmoid(x):
    return 1.0 / (1.0 + jnp.exp(-x))


def _silu(x):
    return x * _sigmoid(x)


def _pack_pair(lo, hi):
    lo_bits = lax.bitcast_convert_type(lo.astype(BF16).astype(F32), I32)
    hi_bits = lax.bitcast_convert_type(hi.astype(BF16).astype(F32), I32)
    return lax.shift_right_logical(lo_bits, 16) | (hi_bits & jnp.int32(-65536))


def _unpack_pair(word):
    lo = lax.bitcast_convert_type(lax.shift_left(word, 16), F32)
    hi = lax.bitcast_convert_type(word & jnp.int32(-65536), F32)
    return lo, hi


def _mod_kernel(c_ref, w_ref, b_ref, o_ref):
    a = _silu(c_ref[...])
    o_ref[0] = jnp.dot(a, w_ref[0], preferred_element_type=F32,
                       precision=lax.Precision.HIGHEST) + b_ref[0]


def _modulation(c_all, ada_w, ada_b):
    n = 6 * D_MODEL
    tn = 1024
    return pl.pallas_call(
        _mod_kernel,
        grid=(DEPTH, n // tn),
        in_specs=[
            pl.BlockSpec((SUBLANES, D_MODEL), lambda l, j: (0, 0)),
            pl.BlockSpec((1, D_MODEL, tn), lambda l, j: (l, 0, j)),
            pl.BlockSpec((1, 1, tn), lambda l, j: (l, 0, j)),
        ],
        out_specs=pl.BlockSpec((1, SUBLANES, tn), lambda l, j: (l, 0, j)),
        out_shape=jax.ShapeDtypeStruct((DEPTH, SUBLANES, n), F32),
        compiler_params=_params(("arbitrary", "arbitrary")),
        name="adaln_mod",
    )(c_all, ada_w, ada_b.reshape(DEPTH, 1, n))


def _normed(x_ref, g_ref, sc_ref, sh_ref):
    x = x_ref[...]
    ms = jnp.mean(x * x, axis=-1, keepdims=True)
    y = x * lax.rsqrt(ms + EPS) * g_ref[...]
    return y * (1.0 + sc_ref[0]) + sh_ref[0]


def _norm_kernel(x_ref, g_ref, sc_ref, sh_ref, h_ref):
    h_ref[...] = _normed(x_ref, g_ref, sc_ref, sh_ref).astype(BF16)


def _norm_router_kernel(x_ref, g_ref, sc_ref, sh_ref, wr_ref, h_ref, h3_ref, lg_ref):
    h = _normed(x_ref, g_ref, sc_ref, sh_ref)
    tm = h.shape[0]
    h_hi = h.astype(BF16)
    h_ref[...] = h_hi
    h_lo = (h - h_hi.astype(F32)).astype(BF16)
    w_hi, w_mid, _ = _split3(wr_ref[...])
    lg_ref[...] = _dot_nt(w_hi, h_hi) + _dot_nt(w_hi, h_lo) + _dot_nt(w_mid, h_hi)
    h_r = h_hi.astype(F32)
    for c in range(TOKEN_TILE_ROWS):
        base = 2 * LANES * c
        word = _pack_pair(h_r[:, base:base + LANES], h_r[:, base + LANES:base + 2 * LANES])
        h3_ref[pl.ds(c, tm, stride=TOKEN_TILE_ROWS), :] = word


def _mod_spec(slot, steps_per_seq, width=D_MODEL):
    return pl.BlockSpec((1, 1, width), lambda i: (6 * (i // steps_per_seq) + slot, 0, 0))


def _norm_mod(x, g, mod3, shift_slot, scale_slot, seq_len, router_wt=None):
    t = x.shape[0]
    tm = 512
    sps = seq_len // tm
    in_specs = [
        pl.BlockSpec((tm, D_MODEL), lambda i: (i, 0)),
        pl.BlockSpec((1, D_MODEL), lambda i: (0, 0)),
        _mod_spec(scale_slot, sps),
        _mod_spec(shift_slot, sps),
    ]
    h_spec = pl.BlockSpec((tm, D_MODEL), lambda i: (i, 0))
    h_shape = jax.ShapeDtypeStruct((t, D_MODEL), BF16)
    if router_wt is None:
        return pl.pallas_call(
            _norm_kernel, grid=(t // tm,), in_specs=in_specs, out_specs=h_spec, out_shape=h_shape,
            compiler_params=_params(("arbitrary",)), name="norm_mod",
        )(x, g, mod3, mod3)
    return pl.pallas_call(
        _norm_router_kernel,
        grid=(t // tm,),
        in_specs=in_specs + [pl.BlockSpec((N_EXPERTS, D_MODEL), lambda i: (0, 0))],
        out_specs=[
            h_spec,
            pl.BlockSpec((tm * TOKEN_TILE_ROWS, LANES), lambda i: (i, 0)),
            pl.BlockSpec((N_EXPERTS, tm), lambda i: (0, i)),
        ],
        out_shape=[
            h_shape,
            jax.ShapeDtypeStruct((t * TOKEN_TILE_ROWS, LANES), I32),
            jax.ShapeDtypeStruct((N_EXPERTS, t), F32),
        ],
        compiler_params=_params(("arbitrary",)),
        name="norm_mod_router",
    )(x, g, mod3, mod3, router_wt)


def _mm_kernel(a_ref, w_ref, o_ref):
    o_ref[...] = _dot(a_ref[...], w_ref[...]).astype(o_ref.dtype)


def _mm_res_kernel(a_ref, w_ref, x_ref, gate_ref, o_ref):
    o_ref[...] = x_ref[...] + gate_ref[0] * _dot(a_ref[...], w_ref[...])


def _norm_mm_kernel(x_ref, g_ref, sc_ref, sh_ref, w_ref, o_ref, h_ref):
    @pl.when(pl.program_id(1) == 0)
    def _():
        h_ref[...] = _normed(x_ref, g_ref, sc_ref, sh_ref).astype(BF16)

    o_ref[...] = _dot(h_ref[...], w_ref[...]).astype(o_ref.dtype)


def _norm_matmul(x, g, mod3, shift_slot, scale_slot, seq_len, w, tm=1024, tn=512):
    m, k = x.shape
    n = w.shape[1]
    sps = seq_len // tm

    def mod_spec(slot):
        return pl.BlockSpec((1, 1, k), lambda i, j: (6 * (i // sps) + slot, 0, 0))

    return pl.pallas_call(
        _norm_mm_kernel,
        grid=(m // tm, n // tn),
        in_specs=[pl.BlockSpec((tm, k), lambda i, j: (i, 0)),
                  pl.BlockSpec((1, k), lambda i, j: (0, 0)),
                  mod_spec(scale_slot), mod_spec(shift_slot),
                  pl.BlockSpec((k, tn), lambda i, j: (0, j))],
        out_specs=pl.BlockSpec((tm, tn), lambda i, j: (i, j)),
        out_shape=jax.ShapeDtypeStruct((m, n), BF16),
        scratch_shapes=[pltpu.VMEM((tm, k), BF16)],
        compiler_params=_params(("arbitrary", "arbitrary")),
        name="norm_matmul",
    )(x, g, mod3, mod3, w)


def _matmul(a, w, out_dtype=BF16, tm=1024, tn=512):
    m, k = a.shape
    n = w.shape[1]
    return pl.pallas_call(
        _mm_kernel,
        grid=(m // tm, n // tn),
        in_specs=[pl.BlockSpec((tm, k), lambda i, j: (i, 0)),
                  pl.BlockSpec((k, tn), lambda i, j: (0, j))],
        out_specs=pl.BlockSpec((tm, tn), lambda i, j: (i, j)),
        out_shape=jax.ShapeDtypeStruct((m, n), out_dtype),
        compiler_params=_params(("arbitrary", "arbitrary")),
        name="matmul",
    )(a, w)


def _matmul_residual(a, w, x, mod3, gate_slot, seq_len, tm=1024, tn=512):
    m, k = a.shape
    n = w.shape[1]
    sps = seq_len // tm
    return pl.pallas_call(
        _mm_res_kernel,
        grid=(m // tm, n // tn),
        in_specs=[pl.BlockSpec((tm, k), lambda i, j: (i, 0)),
                  pl.BlockSpec((k, tn), lambda i, j: (0, j)),
                  pl.BlockSpec((tm, tn), lambda i, j: (i, j)),
                  pl.BlockSpec((1, 1, tn), lambda i, j: (6 * (i // sps) + gate_slot, 0, j))],
        out_specs=pl.BlockSpec((tm, tn), lambda i, j: (i, j)),
        out_shape=jax.ShapeDtypeStruct((m, n), F32),
        compiler_params=_params(("arbitrary", "arbitrary")),
        name="matmul_residual",
    )(a, w, x, mod3)


_ALIBI_SLOPES = [float(s) for s in
                 np.exp2(-8.0 * np.arange(1, N_Q_HEADS + 1, dtype=np.float32) / N_Q_HEADS)]


def _head_norm(x, gain):
    xf = x.astype(F32)
    ms = jnp.mean(xf * xf, axis=-1, keepdims=True)
    return xf * lax.rsqrt(ms + EPS) * gain


def _attn_kernel(sink_ref, q_ref, kp_ref, kc_ref, kn_ref, vp_ref, vc_ref, vn_ref,
                 qg_ref, kg_ref, o_ref, *, n_blocks):
    i = pl.program_id(1)
    span = ATTN_BLOCK + 2 * WINDOW
    t_idx = lax.broadcasted_iota(I32, (ATTN_BLOCK, span), 0)
    j_idx = lax.broadcasted_iota(I32, (ATTN_BLOCK, span), 1)
    dist = jnp.abs(t_idx - j_idx + WINDOW)
    valid = dist <= WINDOW
    valid = valid & ((j_idx >= WINDOW) | (i > 0))
    valid = valid & ((j_idx < WINDOW + ATTN_BLOCK) | (i < n_blocks - 1))
    dist_f = dist.astype(F32)
    q_gain = qg_ref[...] * (HEAD_DIM ** -0.5)
    k_gain = kg_ref[...]
    for g in range(N_KV_HEADS):
        cols = slice(g * HEAD_DIM, (g + 1) * HEAD_DIM)
        k_all = jnp.concatenate([kp_ref[:, cols], kc_ref[:, cols], kn_ref[:, cols]], axis=0)
        v_all = jnp.concatenate([vp_ref[:, cols], vc_ref[:, cols], vn_ref[:, cols]], axis=0)
        k_n = _head_norm(k_all, k_gain).astype(BF16)
        q_n = []
        for jh in range(Q_PER_KV):
            h = g * Q_PER_KV + jh
            q_n.append(_head_norm(q_ref[:, h * HEAD_DIM:(h + 1) * HEAD_DIM], q_gain).astype(BF16))
        s_all = _dot_nt(jnp.concatenate(q_n, axis=0), k_n)
        p_all, inv_den = [], []
        for jh in range(Q_PER_KV):
            h = g * Q_PER_KV + jh
            sink = sink_ref[h]
            s = s_all[jh * ATTN_BLOCK:(jh + 1) * ATTN_BLOCK] - _ALIBI_SLOPES[h] * dist_f
            s = jnp.where(valid, s, NEG_INF)
            m = jnp.maximum(jnp.max(s, axis=-1, keepdims=True), sink)
            p = jnp.exp(s - m)
            den = jnp.sum(p, axis=-1, keepdims=True) + jnp.exp(sink - m)
            p_all.append(p.astype(BF16))
            inv_den.append(1.0 / den)
        o_all = _dot(jnp.concatenate(p_all, axis=0), v_all)
        outs = [o_all[jh * ATTN_BLOCK:(jh + 1) * ATTN_BLOCK] * inv_den[jh] for jh in range(Q_PER_KV)]
        o_ref[:, g * Q_PER_KV * HEAD_DIM:(g + 1) * Q_PER_KV * HEAD_DIM] = (
            jnp.concatenate(outs, axis=1).astype(BF16))


def _attention(qkv, q_gain, k_gain, sink, batch, seq_len):
    t = qkv.shape[0]
    nb = seq_len // ATTN_BLOCK
    qw = N_Q_HEADS * HEAD_DIM
    kvw = N_KV_HEADS * HEAD_DIM
    k_col = qw // kvw
    v_col = k_col + 1

    def rows(b, i):
        return b * nb + i

    def kv_spec(col, shift):
        return pl.BlockSpec(
            (ATTN_BLOCK, kvw),
            lambda b, i: (rows(b, jnp.clip(i + shift, 0, nb - 1)), col))

    return pl.pallas_call(
        functools.partial(_attn_kernel, n_blocks=nb),
        grid=(batch, nb),
        in_specs=[
            pl.BlockSpec(memory_space=pltpu.SMEM),
            pl.BlockSpec((ATTN_BLOCK, qw), lambda b, i: (rows(b, i), 0)),
            kv_spec(k_col, -1), kv_spec(k_col, 0), kv_spec(k_col, 1),
            kv_spec(v_col, -1), kv_spec(v_col, 0), kv_spec(v_col, 1),
            pl.BlockSpec((1, HEAD_DIM), lambda b, i: (0, 0)),
            pl.BlockSpec((1, HEAD_DIM), lambda b, i: (0, 0)),
        ],
        out_specs=pl.BlockSpec((ATTN_BLOCK, qw), lambda b, i: (rows(b, i), 0)),
        out_shape=jax.ShapeDtypeStruct((t, qw), BF16),
        compiler_params=_params(("arbitrary", "arbitrary")),
        name="window_attention",
    )(sink, qkv, qkv, qkv, qkv, qkv, qkv, qkv, q_gain.reshape(1, HEAD_DIM), k_gain.reshape(1, HEAD_DIM))


GLA_GATE_PAD = LANES


def _gla_gate_kernel(h_ref, wlr_ref, wg_ref, bg_ref, laf_ref, lab_ref):
    h = h_ref[...]
    w_hi, w_mid, _ = _split3(wlr_ref[...])
    lr = _dot(h, w_hi) + _dot(h, w_mid)
    l_hi, l_mid, l_lo = _split3(lr)
    g_hi, g_mid, _ = _split3(wg_ref[...])
    z = (_dot(l_hi, g_hi) + _dot(l_mid, g_hi) + _dot(l_hi, g_mid) + _dot(l_lo, g_hi)
         + bg_ref[...])
    log_sig = jnp.minimum(z, 0.0) - jnp.log(1.0 + jnp.exp(-jnp.abs(z)))
    la = log_sig / GLA_TAU
    n = GLA_HEADS * GLA_DK
    laf_ref[...] = la[:, :n]
    lab_ref[...] = la[:, n:]


def _gla_gates(h, w_lr, w_g, b_g):
    t = h.shape[0]
    tm = 512
    n = GLA_HEADS * GLA_DK
    out = jax.ShapeDtypeStruct((t, n), F32)
    return pl.pallas_call(
        _gla_gate_kernel,
        grid=(t // tm,),
        in_specs=[pl.BlockSpec((tm, D_MODEL), lambda i: (i, 0)),
                  pl.BlockSpec((D_MODEL, GLA_GATE_PAD), lambda i: (0, 0)),
                  pl.BlockSpec((GLA_GATE_PAD, 2 * n), lambda i: (0, 0)),
                  pl.BlockSpec((1, 2 * n), lambda i: (0, 0))],
        out_specs=[pl.BlockSpec((tm, n), lambda i: (i, 0)), pl.BlockSpec((tm, n), lambda i: (i, 0))],
        out_shape=[out, out],
        compiler_params=_params(("arbitrary",)),
        name="gla_gates",
    )(h, w_lr, w_g, b_g)


GLA_STEP = 256


def _gla_chunks(q_ref, k_ref, v_ref, la_ref, st_ref, reverse):
    shift = GLA_CHUNK.bit_length() - 1
    r_idx = lax.broadcasted_iota(I32, (GLA_STEP, GLA_STEP), 0)
    c_idx = lax.broadcasted_iota(I32, (GLA_STEP, GLA_STEP), 1)
    same_chunk = lax.shift_right_logical(r_idx, shift) == lax.shift_right_logical(c_idx, shift)
    if reverse:
        sum_mat = same_chunk & (c_idx >= r_idx)
        keep = same_chunk & (c_idx > r_idx)
    else:
        sum_mat = same_chunk & (c_idx <= r_idx)
        keep = sum_mat
    sum_mat = jnp.where(sum_mat, 1.0, 0.0).astype(BF16)
    chunk_mat = jnp.where(same_chunk, 1.0, 0.0).astype(BF16)
    n_chunks = GLA_STEP // GLA_CHUNK
    la_hi, la_mid, la_lo = _split3(la_ref[...])
    b = _dot(sum_mat, la_hi) + _dot(sum_mat, la_mid) + _dot(sum_mat, la_lo)
    total = _dot(chunk_mat, la_hi) + _dot(chunk_mat, la_mid) + _dot(chunk_mat, la_lo)
    q = q_ref[...].astype(F32) * (GLA_DK ** -0.5)
    k = k_ref[...].astype(F32)
    v = v_ref[...]
    q_dec = (q * jnp.exp(b)).astype(BF16)
    k_inv = (k * jnp.exp(-b)).astype(BF16)
    k_end = (k * jnp.exp(total - b)).astype(BF16)
    decay = jnp.exp(total)
    a = jnp.where(keep, _dot_nt(q_dec, k_inv), 0.0).astype(BF16)
    o_intra = _dot(a, v)
    order = range(n_chunks - 1, -1, -1) if reverse else range(n_chunks)
    outs = [None] * n_chunks
    st = st_ref[...]
    for c in order:
        lo = c * GLA_CHUNK
        rows = slice(lo, lo + GLA_CHUNK)
        outs[c] = _dot_nt(q_dec[rows], st.astype(BF16))
        st = st * decay[lo:lo + 1, :] + _dot_tn(v[rows], k_end[rows])
    st_ref[...] = st
    return o_intra + jnp.concatenate(outs, axis=0)


def _gla_fwd_kernel(q_ref, k_ref, v_ref, la_ref, o_ref, st_ref):
    @pl.when(pl.program_id(2) == 0)
    def _():
        st_ref[...] = jnp.zeros_like(st_ref)
    o_ref[...] = _gla_chunks(q_ref, k_ref, v_ref, la_ref, st_ref, reverse=False)


def _gla_bwd_kernel(q_ref, k_ref, v_ref, la_ref, of_ref, r_ref, ng_ref, o_ref, st_ref):
    @pl.when(pl.program_id(2) == 0)
    def _():
        st_ref[...] = jnp.zeros_like(st_ref)
    o = of_ref[...] + _gla_chunks(q_ref, k_ref, v_ref, la_ref, st_ref, reverse=True)
    ms = jnp.mean(o * o, axis=-1, keepdims=True)
    o = o * lax.rsqrt(ms + EPS) * ng_ref[...]
    o_ref[...] = (o * _silu(r_ref[...].astype(F32))).astype(BF16)


def _gla(proj, la_f, la_b, norm_g, batch, seq_len):
    t = proj.shape[0]
    nblk = seq_len // GLA_STEP
    k_col = GLA_HEADS
    v_col = 2 * GLA_HEADS * GLA_DK // GLA_DV
    r_col = v_col + GLA_HEADS

    def specs(row):
        return [
            pl.BlockSpec((GLA_STEP, GLA_DK), lambda b, h, n: (row(b, n), h)),
            pl.BlockSpec((GLA_STEP, GLA_DK), lambda b, h, n: (row(b, n), k_col + h)),
            pl.BlockSpec((GLA_STEP, GLA_DV), lambda b, h, n: (row(b, n), v_col + h)),
            pl.BlockSpec((GLA_STEP, GLA_DK), lambda b, h, n: (row(b, n), h)),
        ]

    def out_spec(row):
        return pl.BlockSpec((GLA_STEP, GLA_DV), lambda b, h, n: (row(b, n), h))

    def fwd_row(b, n):
        return b * nblk + n

    def bwd_row(b, n):
        return b * nblk + (nblk - 1 - n)

    grid = (batch, GLA_HEADS, nblk)
    state = pltpu.VMEM((GLA_DV, GLA_DK), F32)
    sem = ("arbitrary", "arbitrary", "arbitrary")
    o_f = pl.pallas_call(
        _gla_fwd_kernel, grid=grid, in_specs=specs(fwd_row), out_specs=out_spec(fwd_row),
        out_shape=jax.ShapeDtypeStruct((t, GLA_HEADS * GLA_DV), F32),
        scratch_shapes=[state], compiler_params=_params(sem), name="gla_forward",
    )(proj, proj, proj, la_f)
    return pl.pallas_call(
        _gla_bwd_kernel, grid=grid,
        in_specs=specs(bwd_row) + [
            out_spec(bwd_row),
            pl.BlockSpec((GLA_STEP, GLA_DV), lambda b, h, n: (bwd_row(b, n), r_col + h)),
            pl.BlockSpec((1, GLA_DV), lambda b, h, n: (0, 0)),
        ],
        out_specs=out_spec(bwd_row),
        out_shape=jax.ShapeDtypeStruct((t, GLA_HEADS * GLA_DV), BF16),
        scratch_shapes=[state], compiler_params=_params(sem), name="gla_backward",
    )(proj, proj, proj, la_b, o_f, proj, norm_g.reshape(1, GLA_DV))


ROUTE_TILE = 512


def _route_kernel(lg_ref, bias_ref, idx_ref, w_ref, rank_ref, cnt_ref, run_ref, tri_ref):
    tm = ROUTE_TILE

    @pl.when(pl.program_id(0) == 0)
    def _():
        run_ref[...] = jnp.zeros_like(run_ref)
        r = lax.broadcasted_iota(I32, (tm, tm), 0)
        c = lax.broadcasted_iota(I32, (tm, tm), 1)
        tri_ref[...] = (r < c).astype(BF16)

    shape3 = (N_GROUPS, GROUP_SIZE, tm)
    scores = _sigmoid(lg_ref[...])
    biased = (scores + bias_ref[...]).reshape(shape3)
    scores3 = scores.reshape(shape3)
    in_grp = lax.broadcasted_iota(I32, shape3, 1)
    grp_id = lax.broadcasted_iota(I32, shape3, 0)
    exp_id = grp_id * GROUP_SIZE + in_grp

    m1 = jnp.max(biased, axis=1, keepdims=True)
    first = jnp.min(jnp.where(biased == m1, in_grp, GROUP_SIZE), axis=1, keepdims=True)
    m2 = jnp.max(jnp.where(in_grp == first, NEG_INF, biased), axis=1, keepdims=True)
    grp_score = m1 + m2

    g_iota = lax.broadcasted_iota(I32, (N_GROUPS, 1, tm), 0)
    g_sel = jnp.zeros((N_GROUPS, 1, tm), dtype=jnp.bool_)
    cur = grp_score
    for _ in range(TOPK_GROUPS):
        m = jnp.max(cur, axis=0, keepdims=True)
        pick = jnp.min(jnp.where(cur == m, g_iota, N_GROUPS), axis=0, keepdims=True)
        hit = g_iota == pick
        g_sel = g_sel | hit
        cur = jnp.where(hit, NEG_INF, cur)

    cand = jnp.where(g_sel, biased, NEG_INF)
    picks, weights = [], []
    sel = jnp.zeros(shape3, dtype=jnp.bool_)
    for _ in range(TOP_K):
        m = jnp.max(jnp.max(cand, axis=1, keepdims=True), axis=0, keepdims=True)
        pick = jnp.where(cand == m, exp_id, N_EXPERTS)
        pick = jnp.min(jnp.min(pick, axis=1, keepdims=True), axis=0, keepdims=True)
        hit = exp_id == pick
        sel = sel | hit
        cand = jnp.where(hit, NEG_INF, cand)
        w = jnp.where(hit, scores3, 0.0)
        weights.append(jnp.sum(jnp.sum(w, axis=1, keepdims=True), axis=0, keepdims=True))
        picks.append(pick)

    sel_f = jnp.where(sel, 1.0, 0.0).reshape(N_EXPERTS, tm)
    prefix = (_dot(sel_f.astype(BF16), tri_ref[...]) + run_ref[:, 0:1]).reshape(shape3)
    ranks = []
    for pick in picks:
        r = jnp.where(exp_id == pick, prefix, 0.0)
        ranks.append(jnp.sum(jnp.sum(r, axis=1, keepdims=True), axis=0, keepdims=True))
    new_run = run_ref[...] + jnp.sum(sel_f, axis=1, keepdims=True)
    run_ref[...] = new_run
    cnt_ref[...] = new_run.astype(I32)

    total = weights[0]
    for w in weights[1:]:
        total = total + w
    row = lax.broadcasted_iota(I32, (TOP_K, tm), 0)

    def stack(vals):
        out = jnp.zeros((TOP_K, tm), vals[0].dtype)
        for k, v in enumerate(vals):
            out = jnp.where(row == k, v.reshape(1, tm), out)
        return out

    idx_ref[...] = stack(picks)
    w_ref[...] = stack([w / total * ROUTED_SCALE for w in weights])
    rank_ref[...] = stack(ranks).astype(I32)


def _route(logits_t, bias):
    t = logits_t.shape[1]
    tm = ROUTE_TILE
    tok = pl.BlockSpec((TOP_K, tm), lambda i: (0, i))
    idx, w, rank, cnt = pl.pallas_call(
        _route_kernel,
        grid=(t // tm,),
        in_specs=[pl.BlockSpec((N_EXPERTS, tm), lambda i: (0, i)),
                  pl.BlockSpec((N_EXPERTS, 1), lambda i: (0, 0))],
        out_specs=[tok, tok, tok, pl.BlockSpec((N_EXPERTS, LANES), lambda i: (0, 0))],
        out_shape=[jax.ShapeDtypeStruct((TOP_K, t), I32),
                   jax.ShapeDtypeStruct((TOP_K, t), F32),
                   jax.ShapeDtypeStruct((TOP_K, t), I32),
                   jax.ShapeDtypeStruct((N_EXPERTS, LANES), I32)],
        scratch_shapes=[pltpu.VMEM((N_EXPERTS, LANES), F32), pltpu.VMEM((tm, tm), BF16)],
        compiler_params=_params(("arbitrary",)),
        name="moe_route",
    )(logits_t, bias.reshape(N_EXPERTS, 1))
    return idx, w, rank, cnt[:, 0]


DISPATCH_TILE = 512


def _token_rows(i):
    return pl.ds(pl.multiple_of(i * TOKEN_TILE_ROWS, TOKEN_TILE_ROWS), TOKEN_TILE_ROWS)


def _block_rows(b):
    n = SLOT_BLOCK * TOKEN_TILE_ROWS
    return pl.ds(pl.multiple_of(b * n, n), n)


def _dispatch_kernel(cnt_ref, start_ref, nu_ref, pos_ref, h3_ref, x3_ref, zero_ref, zero_tile,
                     sem, pad_sem, *, n_blocks):
    i = pl.program_id(0)
    tm = DISPATCH_TILE

    def issue(t, carry):
        for k in range(TOP_K):
            pltpu.make_async_copy(h3_ref.at[_token_rows(t)],
                                  x3_ref.at[_token_rows(pos_ref[k, t])], sem
                                  ).start(priority=k % N_DMA_PRIORITIES)
        return carry

    lax.fori_loop(0, tm, issue, 0)

    @pl.when(i == 0)
    def _():
        zero_ref[...] = jnp.zeros_like(zero_ref)
        zero_tile[...] = jnp.zeros_like(zero_tile)

        def per_expert(e, n_pad):
            cnt = cnt_ref[e]
            padded = (cnt + SLOT_BLOCK - 1) // SLOT_BLOCK * SLOT_BLOCK

            def pad_row(r, c):
                pltpu.make_async_copy(zero_tile, x3_ref.at[_token_rows(start_ref[e] + r)],
                                      pad_sem).start()
                return c

            lax.fori_loop(cnt, padded, pad_row, 0)
            return n_pad + (padded - cnt)

        n_pad = lax.fori_loop(0, N_EXPERTS, per_expert, 0)

        def tail_block(b, c):
            pltpu.make_async_copy(zero_ref, x3_ref.at[_block_rows(b)], pad_sem).start()
            return c

        lax.fori_loop(nu_ref[0], n_blocks, tail_block, 0)

        def pad_wait(r, c):
            pltpu.make_async_copy(zero_tile, x3_ref.at[_token_rows(0)], pad_sem).wait()
            return c

        lax.fori_loop(0, n_pad, pad_wait, 0)

        def tail_wait(b, c):
            pltpu.make_async_copy(zero_ref, x3_ref.at[_block_rows(0)], pad_sem).wait()
            return c

        lax.fori_loop(nu_ref[0], n_blocks, tail_wait, 0)

    n_rows = tm * TOP_K * TOKEN_TILE_ROWS
    pltpu.make_async_copy(x3_ref.at[pl.ds(0, n_rows)], x3_ref.at[pl.ds(0, n_rows)], sem).wait()


def _dispatch(h3, pos, counts, starts, n_used, n_blocks):
    t = pos.shape[1]
    tm = DISPATCH_TILE
    return pl.pallas_call(
        functools.partial(_dispatch_kernel, n_blocks=n_blocks),
        grid_spec=pltpu.PrefetchScalarGridSpec(
            num_scalar_prefetch=3,
            grid=(t // tm,),
            in_specs=[pl.BlockSpec((TOP_K, tm), lambda i, c, s, u: (0, i), memory_space=pltpu.SMEM),
                      pl.BlockSpec((tm * TOKEN_TILE_ROWS, LANES), lambda i, c, s, u: (i, 0))],
            out_specs=pl.BlockSpec(memory_space=pl.ANY),
            scratch_shapes=[pltpu.VMEM((SLOT_BLOCK * TOKEN_TILE_ROWS, LANES), I32),
                            pltpu.VMEM((TOKEN_TILE_ROWS, LANES), I32),
                            pltpu.SemaphoreType.DMA, pltpu.SemaphoreType.DMA],
        ),
        out_shape=jax.ShapeDtypeStruct((n_blocks * SLOT_BLOCK * TOKEN_TILE_ROWS, LANES), I32),
        compiler_params=_params(("arbitrary",)),
        name="moe_dispatch",
    )(counts, starts, n_used, pos, h3)


def _expert_kernel(be_ref, nu_ref, x3_ref, *refs):
    w_refs, y3_ref = refs[:-1], refs[-1]
    first = pl.program_id(0) * EXPERT_STEP_BLOCKS

    @pl.when(first >= nu_ref[0])
    def _():
        y3_ref[...] = jnp.zeros_like(y3_ref)

    @pl.when(first < nu_ref[0])
    def _():
        m = SLOT_BLOCK
        offsets = [g * m * TOKEN_TILE_ROWS for g in range(EXPERT_STEP_BLOCKS)]
        weights = [w_refs[3 * g:3 * g + 3] for g in range(EXPERT_STEP_BLOCKS)]
        xs = []
        for off in offsets:
            pieces = []
            for c in range(TOKEN_TILE_ROWS):
                lo, hi = _unpack_pair(x3_ref[pl.ds(off + c, m, stride=TOKEN_TILE_ROWS), :])
                pieces += [lo.astype(BF16), hi.astype(BF16)]
            xs.append(jnp.concatenate(pieces, axis=1))
        acts = [(_silu(_dot(x, wg[0, 0])) * _dot(x, wu[0, 0])).astype(BF16)
                for x, (wg, wu, _) in zip(xs, weights)]
        ys = [_dot(a, wd[0, 0]) for a, (_, _, wd) in zip(acts, weights)]
        for off, y in zip(offsets, ys):
            for c in range(TOKEN_TILE_ROWS):
                base = 2 * LANES * c
                y3_ref[pl.ds(off + c, m, stride=TOKEN_TILE_ROWS), :] = _pack_pair(
                    y[:, base:base + LANES], y[:, base + LANES:base + 2 * LANES])


def _experts(x3, block_expert, n_used, layer, w_gate, w_up, w_down):
    n_blocks = block_expert.shape[0]
    assert n_blocks % EXPERT_STEP_BLOCKS == 0
    rows = EXPERT_STEP_BLOCKS * SLOT_BLOCK * TOKEN_TILE_ROWS

    def expert_block(g):
        def index(s, be, nu):
            return (layer, be[jnp.minimum(s * EXPERT_STEP_BLOCKS + g, nu[0] - 1)], 0, 0)
        return index

    w_specs, w_args = [], []
    for g in range(EXPERT_STEP_BLOCKS):
        w_specs += [pl.BlockSpec((1, 1, D_MODEL, EXPERT_DIM), expert_block(g)),
                    pl.BlockSpec((1, 1, D_MODEL, EXPERT_DIM), expert_block(g)),
                    pl.BlockSpec((1, 1, EXPERT_DIM, D_MODEL), expert_block(g))]
        w_args += [w_gate, w_up, w_down]
    return pl.pallas_call(
        _expert_kernel,
        grid_spec=pltpu.PrefetchScalarGridSpec(
            num_scalar_prefetch=2,
            grid=(n_blocks // EXPERT_STEP_BLOCKS,),
            in_specs=[pl.BlockSpec((rows, LANES), lambda s, be, nu: (s, 0))] + w_specs,
            out_specs=pl.BlockSpec((rows, LANES), lambda s, be, nu: (s, 0)),
        ),
        out_shape=jax.ShapeDtypeStruct(x3.shape, I32),
        compiler_params=_params(("arbitrary",)),
        name="moe_experts",
    )(block_expert, n_used, x3, *w_args)


COMBINE_TILE = 256


def _combine_kernel(pos_ref, x_ref, h_ref, w_ref, gate_ref, sg_ref, su_ref, sd_ref, y3_ref,
                    o_ref, buf_ref, sem):
    tm = COMBINE_TILE

    def issue(t, carry):
        for k in range(TOP_K):
            pltpu.make_async_copy(y3_ref.at[_token_rows(pos_ref[t * TOP_K + k])],
                                  buf_ref.at[_token_rows(k * tm + t)], sem
                                  ).start(priority=k % N_DMA_PRIORITIES)
        return carry

    lax.fori_loop(0, tm, issue, 0)

    h = h_ref[...]
    a = (_silu(_dot(h, sg_ref[...])) * _dot(h, su_ref[...])).astype(BF16)
    shared = _dot(a, sd_ref[...])

    pltpu.make_async_copy(y3_ref.at[pl.ds(0, buf_ref.shape[0])], buf_ref, sem).wait()

    gate = gate_ref[0]
    wk = [w_ref[:, k:k + 1] for k in range(TOP_K)]
    for c in range(TOKEN_TILE_ROWS):
        lo_acc = None
        for k in range(TOP_K):
            word = buf_ref[pl.ds(k * tm * TOKEN_TILE_ROWS + c, tm, stride=TOKEN_TILE_ROWS), :]
            lo, hi = _unpack_pair(word)
            if lo_acc is None:
                lo_acc, hi_acc = wk[k] * lo, wk[k] * hi
            else:
                lo_acc, hi_acc = lo_acc + wk[k] * lo, hi_acc + wk[k] * hi
        base = 2 * LANES * c
        for off, acc in ((base, lo_acc), (base + LANES, hi_acc)):
            cols = slice(off, off + LANES)
            o_ref[:, cols] = x_ref[:, cols] + gate[:, cols] * (acc + shared[:, cols])


def _combine(x, h, y3, pos, w_tok, mod3, gate_slot, seq_len, s_gate, s_up, s_down):
    t = x.shape[0]
    tm = COMBINE_TILE
    sps = seq_len // tm
    full = lambda shape: pl.BlockSpec(shape, lambda i: tuple(0 for _ in shape))
    return pl.pallas_call(
        _combine_kernel,
        grid=(t // tm,),
        in_specs=[
            pl.BlockSpec((tm * TOP_K,), lambda i: (i,), memory_space=pltpu.SMEM),
            pl.BlockSpec((tm, D_MODEL), lambda i: (i, 0)),
            pl.BlockSpec((tm, D_MODEL), lambda i: (i, 0)),
            pl.BlockSpec((tm, TOP_K), lambda i: (i, 0)),
            _mod_spec(gate_slot, sps),
            full((D_MODEL, EXPERT_DIM)), full((D_MODEL, EXPERT_DIM)), full((EXPERT_DIM, D_MODEL)),
            pl.BlockSpec(memory_space=pl.ANY),
        ],
        out_specs=pl.BlockSpec((tm, D_MODEL), lambda i: (i, 0)),
        out_shape=jax.ShapeDtypeStruct((t, D_MODEL), F32),
        scratch_shapes=[pltpu.VMEM((TOP_K * tm * TOKEN_TILE_ROWS, LANES), I32),
                        pltpu.SemaphoreType.DMA],
        compiler_params=_params(("arbitrary",)),
        name="moe_combine",
    )(pos, x, h, w_tok, mod3, s_gate, s_up, s_down, y3)


def _moe_layer(x, mod3, norm_g, seq_len, layer, router_wt, router_bias, w_gate, w_up, w_down,
               s_gate, s_up, s_down):
    t = x.shape[0]
    h, h3, logits_t = _norm_mod(x, norm_g, mod3, 3, 4, seq_len, router_wt)
    idx, w, rank, counts = _route(logits_t, router_bias)
    n_blocks = -(-(t * TOP_K + N_EXPERTS * (SLOT_BLOCK - 1)) // SLOT_BLOCK)
    padded = (counts + SLOT_BLOCK - 1) // SLOT_BLOCK * SLOT_BLOCK
    padded_end = jnp.cumsum(padded)
    starts = (padded_end - padded).astype(I32)
    block_start = jnp.arange(n_blocks, dtype=I32) * SLOT_BLOCK
    block_expert = jnp.minimum(
        jnp.sum((padded_end[None, :] <= block_start[:, None]).astype(I32), axis=1), N_EXPERTS - 1)
    n_used = (padded_end[-1:] // SLOT_BLOCK).astype(I32)
    experts = jnp.arange(N_EXPERTS, dtype=I32)
    pos = jnp.sum(jnp.where(idx[:, :, None] == experts, starts, 0), axis=-1) + rank
    x3 = _dispatch(h3, pos, counts.astype(I32), starts, n_used, n_blocks)
    y3 = _experts(x3, block_expert, n_used, layer, w_gate, w_up, w_down)
    return _combine(x, h, y3, pos.T.reshape(-1), w.T, mod3, 5, seq_len, s_gate, s_up, s_down)


def _trunk(x, mod, batch, seq_len, p):
    t = batch * seq_len
    x = x.reshape(t, D_MODEL)
    for layer in range(DEPTH):
        mod3 = mod[layer].reshape(batch * 6, 1, D_MODEL)
        g_mix = p["norm_mix"][layer].reshape(1, D_MODEL)
        g_ffn = p["norm_ffn"][layer].reshape(1, D_MODEL)
        if layer % 2 == 0:
            a = layer // 2
            qkv = _norm_matmul(x, g_mix, mod3, 0, 1, seq_len, p["attn_w_in"][a])
            o = _attention(qkv, p["attn_q_norm"][a], p["attn_k_norm"][a], p["attn_sink"][a],
                           batch, seq_len)
            x = _matmul_residual(o, p["attn_w_out"][a], x, mod3, 2, seq_len)
        else:
            g = layer // 2
            h = _norm_mod(x, g_mix, mod3, 0, 1, seq_len)
            proj = _matmul(h, p["gla_w_main"][g])
            la_f, la_b = _gla_gates(h, p["gla_w_lr"][g], p["gla_w_g"][g], p["gla_b_g"][g])
            o = _gla(proj, la_f, la_b, p["gla_norm"][g], batch, seq_len)
            x = _matmul_residual(o, p["gla_w_out"][g], x, mod3, 2, seq_len)
        x = _moe_layer(x, mod3, g_ffn, seq_len, layer, p["router_wt"][layer], p["router_bias"][layer],
                       p["exp_w_gate"], p["exp_w_up"], p["exp_w_down"],
                       p["sh_w_gate"][layer], p["sh_w_up"][layer], p["sh_w_down"][layer])
    return x.reshape(batch, seq_len, D_MODEL)


def kernel(x_prompt, x_sample, c_prompt, c_sample, ada_w, ada_b, norm_mix, norm_ffn, attn_w_in, attn_q_norm, attn_k_norm, attn_sink, attn_w_out, gla_w_in, gla_w_g2_f, gla_b_g_f, gla_w_g2_b, gla_b_g_b, gla_norm, gla_w_out, router_w, router_bias, exp_w_gate, exp_w_up, exp_w_down, sh_w_gate, sh_w_up, sh_w_down):
    bp, sp, _ = x_prompt.shape
    bs, ss, _ = x_sample.shape
    n_gate = GLA_HEADS * GLA_DK
    n_layers_gla = gla_w_in.shape[0]

    w_lr = jnp.zeros((n_layers_gla, D_MODEL, GLA_GATE_PAD), F32)
    w_lr = w_lr.at[:, :, :2 * GLA_RANK].set(gla_w_in[:, :, GLA_MAIN:])
    w_g = jnp.zeros((n_layers_gla, GLA_GATE_PAD, 2 * n_gate), F32)
    w_g = w_g.at[:, :GLA_RANK, :n_gate].set(gla_w_g2_f)
    w_g = w_g.at[:, GLA_RANK:2 * GLA_RANK, n_gate:].set(gla_w_g2_b)
    p = {
        "norm_mix": norm_mix, "norm_ffn": norm_ffn,
        "attn_w_in": attn_w_in.astype(BF16), "attn_q_norm": attn_q_norm,
        "attn_k_norm": attn_k_norm, "attn_sink": attn_sink, "attn_w_out": attn_w_out.astype(BF16),
        "gla_w_main": gla_w_in[:, :, :GLA_MAIN].astype(BF16), "gla_w_lr": w_lr, "gla_w_g": w_g,
        "gla_b_g": jnp.concatenate([gla_b_g_f, gla_b_g_b], axis=-1).reshape(n_layers_gla, 1, 2 * n_gate),
        "gla_norm": gla_norm, "gla_w_out": gla_w_out.astype(BF16),
        "router_wt": jnp.swapaxes(router_w, 1, 2), "router_bias": router_bias,
        "exp_w_gate": exp_w_gate.astype(BF16), "exp_w_up": exp_w_up.astype(BF16),
        "exp_w_down": exp_w_down.astype(BF16),
        "sh_w_gate": sh_w_gate.astype(BF16), "sh_w_up": sh_w_up.astype(BF16),
        "sh_w_down": sh_w_down.astype(BF16),
    }

    c_all = jnp.zeros((SUBLANES, D_MODEL), F32).at[:bp + bs].set(jnp.concatenate([c_prompt, c_sample], 0))
    mod = _modulation(c_all, ada_w, ada_b)
    y_prompt = _trunk(x_prompt, mod[:, :bp], bp, sp, p)
    y_sample = _trunk(x_sample, mod[:, bp:bp + bs], bs, ss, p)
    return (y_prompt, y_sample)
```
